```python
import jax
import jax.numpy as jnp
from jax import lax
import numpy as np

D_MODEL = 1024
BATCH = 4
SEQ = 8192
DEPTH = 1

GRID_W = 64
CTX_LEN = 256

NA_HEADS = 8
NA_HEAD_DIM = 64
NA_WIDTH = NA_HEADS * NA_HEAD_DIM
WIN_H = 8
WIN_W = 16
ROPE_BASE = 10000.0

RW_HEADS = 8
RW_HEAD_DIM = 64
RW_WIDTH = RW_HEADS * RW_HEAD_DIM
DECAY_LORA = 64
AAA_LORA = 64
GATE_LORA = 128
RW_COLS = 3 * RW_WIDTH + 2 * DECAY_LORA + 2 * AAA_LORA + GATE_LORA
RW_SPLITS = (RW_WIDTH, 2 * RW_WIDTH, 3 * RW_WIDTH,
             3 * RW_WIDTH + DECAY_LORA, 3 * RW_WIDTH + 2 * DECAY_LORA,
             3 * RW_WIDTH + 2 * DECAY_LORA + AAA_LORA, 3 * RW_WIDTH + 2 * DECAY_LORA + 2 * AAA_LORA)

GATE_COLS = 2 * D_MODEL
P_IN = 3 * NA_WIDTH + RW_COLS + GATE_COLS

N_EXPERTS = 16
CAPACITY_FACTOR = 2
D_EXPERT = 2816

ALPHA = (2.0 * DEPTH) ** 0.25
BETA = (8.0 * DEPTH) ** -0.25
LN_EPS = 1e-6
GN_EPS = 64e-5

kernel_name = 'hybrid_natten_rwkv7_ecmoe_dit_block'


def layer_norm(x, gain=None, bias=None, eps=LN_EPS):
    xf = x.astype(jnp.float32)
    mu = jnp.mean(xf, axis=-1, keepdims=True)
    var = jnp.mean(jnp.square(xf - mu), axis=-1, keepdims=True)
    y = (xf - mu) * lax.rsqrt(var + eps)
    if gain is not None:
        y = y * gain.astype(jnp.float32) + bias.astype(jnp.float32)
    return y.astype(x.dtype)


def modulate(h, shift, scale):
    return layer_norm(h) * (1 + scale) + shift


def axial_rope(x):
    B, N, H, Dh = x.shape
    nf = Dh // 4
    t = jnp.arange(N)
    pos = jnp.stack([t // GRID_W, t % GRID_W], axis=-1).astype(jnp.float32)
    inv_freq = jnp.power(jnp.float32(ROPE_BASE), -jnp.arange(nf, dtype=jnp.float32) / nf)
    ang = pos[:, :, None] * inv_freq
    cos = jnp.cos(ang)[None, :, None]
    sin = jnp.sin(ang)[None, :, None]
    xr = x.astype(jnp.float32).reshape(B, N, H, 2, 2, nf)
    x1, x2 = xr[..., 0, :], xr[..., 1, :]
    out = jnp.stack([x1 * cos - x2 * sin, x2 * cos + x1 * sin], axis=-2)
    return out.reshape(B, N, H, Dh).astype(x.dtype)


def token_shift(z, mu_prev, mu_next):
    z_prev = jnp.pad(z[:, :-1], ((0, 0), (1, 0), (0, 0)))
    z_next = jnp.pad(z[:, 1:], ((0, 0), (0, 1), (0, 0)))
    return z + mu_prev * (z_prev - z) + mu_next * (z_next - z)


def neighbourhood_attention(q, k, v, k_ctx, v_ctx, rpb):
    B, H, N, Dh = q.shape
    rows = N // GRID_W
    kh = min(WIN_H, rows)
    scale = Dh ** -0.5
    grid = lambda t: t.reshape(B, H, rows, GRID_W, Dh)
    q, k, v = grid(q), grid(k), grid(v)
    col = jnp.arange(GRID_W)
    col_start = jnp.clip(col - WIN_W // 2, 0, GRID_W - WIN_W)
    col_idx = col_start[:, None] + jnp.arange(WIN_W)[None, :]
    col_off = col_idx - col[:, None] + (WIN_W - 1)

    def row_block(i):
        rs = jnp.clip(i - kh // 2, 0, rows - kh)
        qi = lax.dynamic_index_in_dim(q, i, axis=2, keepdims=False)
        kb = lax.dynamic_slice_in_dim(k, rs, kh, axis=2)[:, :, :, col_idx]
        vb = lax.dynamic_slice_in_dim(v, rs, kh, axis=2)[:, :, :, col_idx]
        row_off = rs + jnp.arange(kh) - i + (WIN_H - 1)
        bias = jnp.transpose(rpb[:, row_off][:, :, col_off], (0, 2, 1, 3))
        s_loc = jnp.einsum('bhqd,bhrqcd->bhqrc', qi, kb).astype(jnp.float32) * scale + bias.astype(jnp.float32)
        s_ctx = jnp.einsum('bhqd,bhld->bhql', qi, k_ctx).astype(jnp.float32) * scale
        s = jnp.concatenate([s_loc.reshape(B, H, GRID_W, kh * WIN_W), s_ctx], axis=-1)
        p = jax.nn.softmax(s, axis=-1).astype(v.dtype)
        p_loc = p[..., :kh * WIN_W].reshape(B, H, GRID_W, kh, WIN_W)
        return (jnp.einsum('bhqrc,bhrqcd->bhqd', p_loc, vb)
                + jnp.einsum('bhql,bhld->bhqd', p[..., kh * WIN_W:], v_ctx))

    out = lax.map(row_block, jnp.arange(rows))
    return jnp.transpose(out, (1, 0, 3, 2, 4)).reshape(B, N, H * Dh)


def context_attention(qc, kc, vc):
    B, H, L, Dh = qc.shape
    s = jnp.einsum('bhqd,bhkd->bhqk', qc, kc).astype(jnp.float32) * Dh ** -0.5
    p = jax.nn.softmax(s, axis=-1).astype(vc.dtype)
    o = jnp.einsum('bhqk,bhkd->bhqd', p, vc)
    return jnp.transpose(o, (0, 2, 1, 3)).reshape(B, L, H * Dh)


def rwkv_features(zr, p):
    B, T, _ = zr.shape
    hd = lambda t: t.reshape(B, T, RW_HEADS, RW_HEAD_DIM)
    r, k, v, wdf, wdb, adf, adb, gd = jnp.split(zr, RW_SPLITS, axis=-1)
    kkf = hd(k * p['k_k']).astype(jnp.float32)
    kk = (kkf * lax.rsqrt(jnp.sum(jnp.square(kkf), axis=-1, keepdims=True) + 1e-12)).astype(k.dtype)
    dirs = []
    for d, (wd, ad) in enumerate(((wdf, adf), (wdb, adb))):
        w_raw = (p['w0'][d] + jnp.tanh(wd) @ p['w_up'][d]).astype(jnp.float32)
        decay = jnp.exp(-jnp.exp(-jax.nn.softplus(-w_raw) - 0.5))
        a = jax.nn.sigmoid(p['a0'][d] + ad @ p['a_up'][d])
        kmod = k * (1 + (a - 1) * p['k_a'])
        dirs.append((hd(decay), hd(kmod), hd(a)))
    g = jax.nn.sigmoid(gd) @ p['g_up']
    return hd(r), hd(v), kk, dirs, g


def rwkv_scan(r, decay, k, a, v, kk, S0, reverse, emit):
    xs = tuple(jnp.moveaxis(t.astype(jnp.float32), 1, 0) for t in (r, decay, k, a, v, kk))

    def step(S, inp):
        r_t, w_t, k_t, a_t, v_t, kk_t = inp
        sa = jnp.einsum('bhvk,bhk->bhv', S, -kk_t)
        S = (S * w_t[:, :, None, :] + sa[..., None] * (kk_t * a_t)[:, :, None, :]
             + v_t[..., None] * k_t[:, :, None, :])
        y = jnp.einsum('bhvk,bhk->bhv', S, r_t) if emit else None
        return S, y

    S, ys = lax.scan(step, S0, xs, reverse=reverse)
    return S, (jnp.moveaxis(ys, 0, 1) if emit else None)


def rwkv_readout(y_sum, r, v, kmods, g, p):
    B, T = r.shape[:2]
    gn_w = p['gn_w'].reshape(RW_HEADS, RW_HEAD_DIM)
    gn_b = p['gn_b'].reshape(RW_HEADS, RW_HEAD_DIM)
    y = layer_norm(y_sum, gn_w, gn_b, GN_EPS).astype(r.dtype)
    for km in kmods:
        y = y + jnp.sum(r * km * p['r_k'], axis=-1, keepdims=True) * v
    return y.reshape(B, T, RW_WIDTH) * g


def merge_branches(y_na, y_rw, z_gate, p):
    gate_na, gate_rw = jnp.split(jax.nn.sigmoid(z_gate), 2, axis=-1)
    return (gate_na * (y_na @ p['w_pa']) + gate_rw * (y_rw @ p['w_pr'])) @ p['w_o']


def token_mixer(u, uc, p, emit_ctx):
    B = u.shape[0]
    cut = [3 * NA_WIDTH, 3 * NA_WIDTH + RW_COLS]
    z_na, z_rw, z_gate = jnp.split(u @ p['w_in'], cut, axis=-1)
    zc_na, zc_rw, zc_gate = jnp.split(uc @ p['w_in'], cut, axis=-1)

    heads = lambda t: t.reshape(t.shape[0], t.shape[1], NA_HEADS, NA_HEAD_DIM)
    bhnd = lambda t: jnp.transpose(t, (0, 2, 1, 3))
    q, k, v = (heads(t) for t in jnp.split(z_na, 3, axis=-1))
    q, k = axial_rope(q), axial_rope(k)
    qc, kc, vc = (bhnd(heads(t)) for t in jnp.split(zc_na, 3, axis=-1))
    y_na = neighbourhood_attention(bhnd(q), bhnd(k), bhnd(v), kc, vc, p['rpb'])

    r, vr, kk, dirs, g = rwkv_features(token_shift(z_rw, p['mu_prev'], p['mu_next']), p)
    rc, vrc, kkc, dirs_c, gc = rwkv_features(token_shift(zc_rw, p['mu_prev'], p['mu_next']), p)
    ys, ycs = [], []
    for d in range(2):
        rev = d == 1
        S0 = jnp.zeros((B, RW_HEADS, RW_HEAD_DIM, RW_HEAD_DIM), jnp.float32)
        S_ctx, y_c = rwkv_scan(rc, *dirs_c[d], vrc, kkc, S0, rev, emit_ctx)
        _, y_l = rwkv_scan(r, *dirs[d], vr, kk, S_ctx, rev, True)
        ys.append(y_l)
        ycs.append(y_c)
    y_rw = rwkv_readout(ys[0] + ys[1], r, vr, [dirs[0][1], dirs[1][1]], g, p)
    out = merge_branches(y_na, y_rw, z_gate, p)

    out_c = None
    if emit_ctx:
        yc_na = context_attention(qc, kc, vc)
        yc_rw = rwkv_readout(ycs[0] + ycs[1], rc, vrc, [dirs_c[0][1], dirs_c[1][1]], gc, p)
        out_c = merge_branches(yc_na, yc_rw, zc_gate, p)
    return out, out_c


def expert_choice_ffn(u, w_router, w1, w3, w2):
    B, T, D = u.shape
    cap = CAPACITY_FACTOR * T // N_EXPERTS
    aff = jax.nn.softmax((u @ w_router).astype(jnp.float32), axis=-1)
    gate, idx = lax.top_k(jnp.swapaxes(aff, 1, 2), cap)
    xs = jax.vmap(lambda ub, ib: ub[ib])(u, idx)
    hdn = jax.nn.silu(jnp.einsum('becd,edf->becf', xs, w1)) * jnp.einsum('becd,edf->becf', xs, w3)
    ye = jnp.einsum('becf,efd->becd', hdn, w2) * gate[..., None].astype(u.dtype)
    return jax.vmap(lambda ib, yb: jnp.zeros((T, D), yb.dtype).at[ib.reshape(-1)].add(yb.reshape(-1, D)))(idx, ye)


def setup_inputs(seed: int = 0) -> dict:
    key = jax.random.key(seed)
    keys = iter(jax.random.split(key, 40))
    D = D_MODEL

    def nrm(shape, s):
        return s * jax.random.normal(next(keys), shape, jnp.float32)

    def unif(shape, hi):
        return jax.random.uniform(next(keys), shape, jnp.float32, 0.0, hi)

    speed = -7.0 + 5.0 * (jnp.arange(RW_WIDTH, dtype=jnp.float32) / (RW_WIDTH - 1)) ** 0.85
    return {
        'x': nrm((BATCH, SEQ, D), 1.0),
        'c': nrm((BATCH, D), 1.0),
        'ctx': nrm((BATCH, CTX_LEN, D), 1.0),
        'c_ctx': nrm((D,), 1.0),
        'w_mod': nrm((DEPTH, D, 6 * D), 0.5 * D ** -0.5),
        'b_mod': nrm((DEPTH, 6 * D), 0.02),
        'w_in': nrm((DEPTH, D, P_IN), D ** -0.5),
        'rpb': nrm((DEPTH, NA_HEADS, 2 * WIN_H - 1, 2 * WIN_W - 1), 0.5),
        'mu_prev': unif((DEPTH, RW_COLS), 0.5),
        'mu_next': unif((DEPTH, RW_COLS), 0.5),
        'w0': speed + 0.5 + nrm((DEPTH, 2, RW_WIDTH), 0.1),
        'w_up': nrm((DEPTH, 2, DECAY_LORA, RW_WIDTH), 0.1 * DECAY_LORA ** -0.5),
        'a0': nrm((DEPTH, 2, RW_WIDTH), 0.1),
        'a_up': nrm((DEPTH, 2, AAA_LORA, RW_WIDTH), 0.5 * AAA_LORA ** -0.5),
        'g_up': nrm((DEPTH, GATE_LORA, RW_WIDTH), GATE_LORA ** -0.5),
        'k_k': 0.85 + nrm((DEPTH, RW_WIDTH), 0.02),
        'k_a': 1.0 + nrm((DEPTH, RW_WIDTH), 0.02),
        'r_k': nrm((DEPTH, RW_HEADS, RW_HEAD_DIM), 0.1),
        'gn_w': 1.0 + nrm((DEPTH, RW_WIDTH), 0.02),
        'gn_b': nrm((DEPTH, RW_WIDTH), 0.02),
        'w_pa': nrm((DEPTH, NA_WIDTH, D), NA_WIDTH ** -0.5),
        'w_pr': nrm((DEPTH, RW_WIDTH, D), RW_WIDTH ** -0.5),
        'w_o': nrm((DEPTH, D, D), BETA * D ** -0.5),
        'ln1_g': 1.0 + nrm((DEPTH, D), 0.02),
        'ln1_b': nrm((DEPTH, D), 0.02),
        'w_router': nrm((DEPTH, D, N_EXPERTS), D ** -0.5),
        'w_e1': nrm((DEPTH, N_EXPERTS, D, D_EXPERT), D ** -0.5),
        'w_e3': nrm((DEPTH, N_EXPERTS, D, D_EXPERT), D ** -0.5),
        'w_e2': nrm((DEPTH, N_EXPERTS, D_EXPERT, D), BETA * D_EXPERT ** -0.5),
        'ln2_g': 1.0 + nrm((DEPTH, D), 0.02),
        'ln2_b': nrm((DEPTH, D), 0.02),
    }


def reference(x, c, ctx, c_ctx, w_mod, b_mod, w_in, rpb, mu_prev, mu_next, w0, w_up, a0, a_up, g_up,
              k_k, k_a, r_k, gn_w, gn_b, w_pa, w_pr, w_o, ln1_g, ln1_b, w_router, w_e1, w_e3, w_e2,
              ln2_g, ln2_b):
    h, hc = x, ctx
    for l in range(DEPTH):
        emit_ctx = l < DEPTH - 1
        p = {'w_in': w_in[l], 'rpb': rpb[l], 'mu_prev': mu_prev[l], 'mu_next': mu_next[l],
             'w0': w0[l], 'w_up': w_up[l], 'a0': a0[l], 'a_up': a_up[l], 'g_up': g_up[l],
             'k_k': k_k[l], 'k_a': k_a[l], 'r_k': r_k[l], 'gn_w': gn_w[l], 'gn_b': gn_b[l],
             'w_pa': w_pa[l], 'w_pr': w_pr[l], 'w_o': w_o[l]}
        mod = jax.nn.silu(c) @ w_mod[l] + b_mod[l]
        mod_c = jax.nn.silu(c_ctx) @ w_mod[l] + b_mod[l]
        sh1, sc1, gt1, sh2, sc2, gt2 = jnp.split(mod[:, None, :], 6, axis=-1)
        csh1, csc1, cgt1, csh2, csc2, cgt2 = jnp.split(mod_c, 6, axis=-1)

        m, mc = token_mixer(modulate(h, sh1, sc1), modulate(hc, csh1, csc1), p, emit_ctx)
        h = layer_norm(ALPHA * h + gt1 * m, ln1_g[l], ln1_b[l])
        moe = expert_choice_ffn(modulate(h, sh2, sc2), w_router[l], w_e1[l], w_e3[l], w_e2[l])
        h = layer_norm(ALPHA * h + gt2 * moe, ln2_g[l], ln2_b[l])
        if emit_ctx:
            hc = layer_norm(ALPHA * hc + cgt1 * mc, ln1_g[l], ln1_b[l])
            moe_c = expert_choice_ffn(modulate(hc, csh2, csc2), w_router[l], w_e1[l], w_e3[l], w_e2[l])
            hc = layer_norm(ALPHA * hc + cgt2 * moe_c, ln2_g[l], ln2_b[l])
    return h
```

```python
import functools
import math

import jax
import jax.numpy as jnp
import numpy as np
from jax import lax
from jax.experimental import pallas as pl
from jax.experimental.pallas import tpu as pltpu

F32 = jnp.float32
BF16 = jnp.bfloat16
HIGHEST = lax.Precision.HIGHEST

GRID_W = 64
NA_HEADS = 8
HEAD_DIM = 64
NA_WIDTH = NA_HEADS * HEAD_DIM
WIN_H = 8
WIN_W = 16
ROPE_BASE = 10000.0
RW_WIDTH = 512
DECAY_LORA = 64
AAA_LORA = 64
GATE_LORA = 128
RW_COLS = 3 * RW_WIDTH + 2 * DECAY_LORA + 2 * AAA_LORA + GATE_LORA
N_EXPERTS = 16
CAPACITY_FACTOR = 2
DEPTH = 1
ALPHA = (2.0 * DEPTH) ** 0.25
LN_EPS = 1e-6
GN_EPS = 64e-5
NEG_BIG = -1e30

LANES = 128
CHUNK = 64
ROW_TILE = 256
NA_ROWS = 4
FFN_TILE = 256
VMEM_LIMIT = 48 * 1024 * 1024


def _cparams(sem):
    return pltpu.CompilerParams(dimension_semantics=sem, vmem_limit_bytes=VMEM_LIMIT)


def _dot(a, b, prec=None):
    return jnp.dot(a, b, preferred_element_type=F32, precision=prec)


def _dot_nt(a, b, prec=None):
    return lax.dot_general(a, b, (((1,), (1,)), ((), ())), preferred_element_type=F32, precision=prec)


def _ln(x, eps):
    mu = jnp.mean(x, axis=-1, keepdims=True)
    xc = x - mu
    var = jnp.mean(xc * xc, axis=-1, keepdims=True)
    return xc * lax.rsqrt(var + eps)


def _head_sum(x, ones_bd):
    hi = x.astype(BF16)
    lo = (x - hi.astype(F32)).astype(BF16)
    return _dot(hi, ones_bd) + _dot(lo, ones_bd)


def _mod_kernel(c_ref, w_ref, b_ref, o_ref):
    cv = c_ref[...]
    s = cv * jax.nn.sigmoid(cv)
    o_ref[...] = _dot(s, w_ref[...], HIGHEST) + b_ref[...]


def _mod_call(cvec, w_mod, b_mod):
    rows, d = cvec.shape
    n = w_mod.shape[1]
    tn = 1536
    return pl.pallas_call(
        _mod_kernel,
        grid=(n // tn,),
        in_specs=[pl.BlockSpec((rows, d), lambda j: (0, 0)),
                  pl.BlockSpec((d, tn), lambda j: (0, j)),
                  pl.BlockSpec((1, tn), lambda j: (0, j))],
        out_specs=pl.BlockSpec((rows, tn), lambda j: (0, j)),
        out_shape=jax.ShapeDtypeStruct((rows, n), F32),
        compiler_params=_cparams(("parallel",)),
        name="mod",
    )(cvec, w_mod, b_mod)


def _rope(z, cos, sin, x1_lane):
    outs = []
    for j in range(z.shape[1] // LANES):
        zj = z[:, j * LANES:(j + 1) * LANES]
        partner = jnp.where(x1_lane, pltpu.roll(zj, LANES - 16, 1), pltpu.roll(zj, 16, 1))
        outs.append(zj * cos + partner * sin)
    return jnp.concatenate(outs, axis=1)


def _proj_kernel(x_ref, sh_ref, sc_ref, w_ref, cos_ref, sin_ref, q_ref, k_ref, v_ref, rw_ref, g_ref):
    x = x_ref[0]
    u = (_ln(x, LN_EPS) * (1.0 + sc_ref[0]) + sh_ref[0]).astype(BF16)
    cos = cos_ref[...]
    sin = sin_ref[...]
    lane = lax.broadcasted_iota(jnp.int32, cos.shape, 1)
    x1_lane = (lane % 32) < 16
    nw = NA_WIDTH
    zq = _dot(u, w_ref[:, 0:nw])
    q_ref[0] = (_rope(zq, cos, sin, x1_lane) * (HEAD_DIM ** -0.5)).astype(BF16)
    zk = _dot(u, w_ref[:, nw:2 * nw])
    k_ref[0] = _rope(zk, cos, sin, x1_lane).astype(BF16)
    v_ref[0] = _dot(u, w_ref[:, 2 * nw:3 * nw]).astype(BF16)
    rw_ref[0] = _dot(u, w_ref[:, 3 * nw:3 * nw + RW_COLS])
    g_ref[0] = _dot(u, w_ref[:, 3 * nw + RW_COLS:]).astype(BF16)


def _proj_call(xall, sh_tab, sc_tab, w_in_bf, cos_tab, sin_tab, n_ctx_tiles):
    b, t, d = xall.shape
    p_in = w_in_bf.shape[1]
    gate_cols = p_in - 3 * NA_WIDTH - RW_COLS
    tm = ROW_TILE
    mod_idx = lambda bi, i: (jnp.where(i < n_ctx_tiles, b, bi), 0, 0)
    row_blk = lambda w: pl.BlockSpec((1, tm, w), lambda bi, i: (bi, i, 0))
    return pl.pallas_call(
        _proj_kernel,
        grid=(b, t // tm),
        in_specs=[row_blk(d),
                  pl.BlockSpec((1, 1, d), mod_idx),
                  pl.BlockSpec((1, 1, d), mod_idx),
                  pl.BlockSpec((d, p_in), lambda bi, i: (0, 0)),
                  pl.BlockSpec((tm, LANES), lambda bi, i: (i, 0)),
                  pl.BlockSpec((tm, LANES), lambda bi, i: (i, 0))],
        out_specs=[row_blk(NA_WIDTH), row_blk(NA_WIDTH), row_blk(NA_WIDTH), row_blk(RW_COLS), row_blk(gate_cols)],
        out_shape=[jax.ShapeDtypeStruct((b, t, NA_WIDTH), BF16)] * 3
        + [jax.ShapeDtypeStruct((b, t, RW_COLS), F32), jax.ShapeDtypeStruct((b, t, gate_cols), BF16)],
        compiler_params=_cparams(("parallel", "parallel")),
        name="proj",
    )(xall, sh_tab, sc_tab, w_in_bf, cos_tab, sin_tab)


def _natten_kernel(q_ref, k_ref, v_ref, bias_ref, o_ref, *, n_ctx, n_rows):
    r = pl.program_id(2)
    kc = k_ref[0, 0:n_ctx, :]
    vc = v_ref[0, 0:n_ctx, :]
    lane2 = lax.broadcasted_iota(jnp.int32, (GRID_W, LANES), 1)
    head0 = lane2 < HEAD_DIM
    for ii in range(NA_ROWS):
        i = r * NA_ROWS + ii
        rs = jnp.clip(i - WIN_H // 2, 0, n_rows - WIN_H)
        off = rs - i + (WIN_H - 1)
        q = q_ref[0, ii * GRID_W:(ii + 1) * GRID_W, :]
        zero = jnp.zeros_like(q)
        q2 = jnp.concatenate([jnp.where(head0, q, zero), jnp.where(head0, zero, q)], axis=0)
        start = pl.multiple_of(n_ctx + rs * GRID_W, GRID_W)
        kw = k_ref[0, pl.ds(start, WIN_H * GRID_W), :]
        vw = v_ref[0, pl.ds(start, WIN_H * GRID_W), :]
        s_loc = _dot_nt(q2, kw) + bias_ref[0, off]
        s_ctx = _dot_nt(q2, kc)
        m = jnp.maximum(jnp.max(s_loc, axis=-1, keepdims=True), jnp.max(s_ctx, axis=-1, keepdims=True))
        p_loc = jnp.exp(s_loc - m)
        p_ctx = jnp.exp(s_ctx - m)
        denom = jnp.sum(p_loc, axis=-1, keepdims=True) + jnp.sum(p_ctx, axis=-1, keepdims=True)
        o2 = (_dot(p_loc.astype(BF16), vw) + _dot(p_ctx.astype(BF16), vc)) / denom
        out = jnp.where(head0, o2[0:GRID_W], o2[GRID_W:2 * GRID_W])
        o_ref[0, ii * GRID_W:(ii + 1) * GRID_W, :] = out.astype(o_ref.dtype)


def _natten_call(q_all, k_all, v_all, bias2, n_ctx, n_lat):
    b, t, _ = q_all.shape
    n_rows = n_lat // GRID_W
    blk = NA_ROWS * GRID_W
    ctx_blocks = n_ctx // blk
    n_pairs = NA_HEADS // 2
    kv_spec = pl.BlockSpec((1, t, LANES), lambda bi, hp, r: (bi, 0, hp))
    return pl.pallas_call(
        functools.partial(_natten_kernel, n_ctx=n_ctx, n_rows=n_rows),
        grid=(b, n_pairs, n_rows // NA_ROWS),
        in_specs=[pl.BlockSpec((1, blk, LANES), lambda bi, hp, r: (bi, ctx_blocks + r, hp)),
                  kv_spec, kv_spec,
                  pl.BlockSpec((1, WIN_H, 2 * GRID_W, WIN_H * GRID_W), lambda bi, hp, r: (hp, 0, 0, 0))],
        out_specs=pl.BlockSpec((1, blk, LANES), lambda bi, hp, r: (bi, r, hp)),
        out_shape=jax.ShapeDtypeStruct((b, n_lat, NA_WIDTH), BF16),
        compiler_params=_cparams(("parallel", "parallel", "arbitrary")),
        name="natten",
    )(q_all, k_all, v_all, bias2)


def _natten_bias_table(rpb):
    col = np.arange(GRID_W)
    col_start = np.clip(col - WIN_W // 2, 0, GRID_W - WIN_W)
    in_win = (col[None, :] >= col_start[:, None]) & (col[None, :] < col_start[:, None] + WIN_W)
    col_off = np.clip(col[None, :] - col[:, None] + (WIN_W - 1), 0, 2 * WIN_W - 2)
    row_off = np.arange(WIN_H)[:, None] + np.arange(WIN_H)[None, :]
    tab = rpb[:, row_off][:, :, :, col_off]
    tab = jnp.where(jnp.asarray(in_win)[None, None, None], tab, NEG_BIG)
    tab = jnp.transpose(tab, (0, 1, 3, 2, 4))
    h = rpb.shape[0]
    tab = tab.reshape(h // 2, 2, WIN_H, GRID_W, WIN_H * GRID_W)
    tab = jnp.transpose(tab, (0, 2, 1, 3, 4)).reshape(h // 2, WIN_H, 2 * GRID_W, WIN_H * GRID_W)
    return tab.astype(F32)


def _feat_kernel(z_ref, prev_ref, next_ref, mup_ref, mun_ref, kk_ref, ka_ref, rk_ref, w0_ref, wup_ref, a0_ref,
                 aup_ref, gup_ref, ones_ref,
                 r_o, v_o, kk_o, lwf_o, lwb_o, kmf_o, kmb_o, af_o, ab_o, g_o, bonus_o):
    z = z_ref[0]
    tm = z.shape[0]
    row = lax.broadcasted_iota(jnp.int32, (tm, 1), 0)
    zp = jnp.where(row == 0, prev_ref[0, 0], pltpu.roll(z, 1, 0))
    zn = jnp.where(row == tm - 1, next_ref[0, 0], pltpu.roll(z, tm - 1, 0))
    zs = z + mup_ref[...] * (zp - z) + mun_ref[...] * (zn - z)
    w = RW_WIDTH
    r = zs[:, 0:w]
    k = zs[:, w:2 * w]
    v = zs[:, 2 * w:3 * w]
    wd = jnp.tanh(zs[:, 3 * w:3 * w + 2 * DECAY_LORA])
    ad = zs[:, 3 * w + 2 * DECAY_LORA:3 * w + 2 * DECAY_LORA + 2 * AAA_LORA].astype(BF16)
    gd = jax.nn.sigmoid(zs[:, 3 * w + 2 * DECAY_LORA + 2 * AAA_LORA:]).astype(BF16)
    ones_bd = ones_ref[...]
    kkf = k * kk_ref[...]
    kk = kkf * lax.rsqrt(_head_sum(kkf * kkf, ones_bd) + 1e-12)
    r_o[0] = r
    v_o[0] = v
    kk_o[0] = kk
    g_o[0] = _dot(gd, gup_ref[...])
    rk = r * rk_ref[...]
    acc = jnp.zeros_like(r)
    for d, (lw_o, km_o, a_o) in enumerate(((lwf_o, kmf_o, af_o), (lwb_o, kmb_o, ab_o))):
        w_raw = w0_ref[d] + _dot(wd, wup_ref[d], HIGHEST)
        lw_o[0] = -math.exp(-0.5) * jax.nn.sigmoid(w_raw)
        a = jax.nn.sigmoid(a0_ref[d] + _dot(ad, aup_ref[d]))
        kmod = k * (1.0 + (a - 1.0) * ka_ref[...])
        a_o[0] = a
        km_o[0] = kmod
        acc = acc + rk * kmod
    bonus_o[0] = _head_sum(acc, ones_bd) * v


def _feat_call(zrw, prev_rows, next_rows, mu_prev, mu_next, k_k, k_a, r_k, w0, wup_pad, a0, aup_pad, g_up, ones_bd):
    b, t, c = zrw.shape
    tm = ROW_TILE
    w = RW_WIDTH
    full = lambda shape: pl.BlockSpec(shape, lambda bi, i: (0,) * len(shape))
    row = pl.BlockSpec((1, tm, w), lambda bi, i: (bi, i, 0))
    edge = pl.BlockSpec((1, 1, 1, c), lambda bi, i: (bi, i, 0, 0))
    return pl.pallas_call(
        _feat_kernel,
        grid=(b, t // tm),
        in_specs=[pl.BlockSpec((1, tm, c), lambda bi, i: (bi, i, 0)), edge, edge,
                  full((1, c)), full((1, c)), full((1, w)), full((1, w)), full((1, w)),
                  full((2, 1, w)), full((2, 2 * DECAY_LORA, w)), full((2, 1, w)), full((2, 2 * AAA_LORA, w)),
                  full((GATE_LORA, w)), full((w, w))],
        out_specs=[row] * 11,
        out_shape=[jax.ShapeDtypeStruct((b, t, w), F32)] * 11,
        compiler_params=_cparams(("parallel", "parallel")),
        name="feat",
    )(zrw, prev_rows, next_rows, mu_prev, mu_next, k_k, k_a, r_k, w0, wup_pad, a0, aup_pad, g_up, ones_bd)


def _stack_bd(x, head0):
    zero = jnp.zeros_like(x)
    return jnp.concatenate([jnp.where(head0, x, zero), jnp.where(head0, zero, x)], axis=0)


def _unstack_bd(x):
    c = x.shape[0] // 2
    return x[0:c] + x[c:2 * c]


def _prep_kernel(r_ref, v_ref, kk_ref, lwf_ref, lwb_ref, kmf_ref, kmb_ref, af_ref, ab_ref,
                 g1_o, g2_o, q1_o, q2_o):
    c = CHUNK
    r = r_ref[0]
    v = v_ref[0]
    kk = kk_ref[0]
    lane = lax.broadcasted_iota(jnp.int32, (c, LANES), 1)
    head0 = lane < HEAD_DIM
    ti = lax.broadcasted_iota(jnp.int32, (c, c), 0)
    si = lax.broadcasted_iota(jnp.int32, (c, c), 1)
    row2 = lax.broadcasted_iota(jnp.int32, (2 * c, 2 * c), 0)
    col2 = lax.broadcasted_iota(jnp.int32, (2 * c, 2 * c), 1)
    t2 = row2 % c
    s2 = col2 % c
    v_bd = _stack_bd(v, head0).astype(BF16)
    for d, (lw_ref, km_ref, a_ref) in enumerate(((lwf_ref, kmf_ref, af_ref), (lwb_ref, kmb_ref, ab_ref))):
        lw = lw_ref[0]
        kmod = km_ref[0]
        ag = a_ref[0]
        if d == 0:
            tri = (si <= ti).astype(F32)
            past = s2 < t2
            last = c - 1
        else:
            tri = (si >= ti).astype(F32)
            past = s2 > t2
            last = 0
        incl = past | (s2 == t2)
        cs = _dot(tri, lw, HIGHEST)
        e_pos = jnp.exp(cs)
        e_neg = jnp.exp(-cs)
        a_t = -kk * jnp.exp(cs - lw)
        b_t = kk * ag * e_neg
        k_t = kmod * e_neg
        r_t = r * e_pos
        wc = e_pos[last:last + 1, :]
        a_bd = _stack_bd(a_t, head0)
        r_bd = _stack_bd(r_t, head0)
        b_bd = _stack_bd(b_t, head0)
        k_bd = _stack_bd(k_t, head0)
        lhs = jnp.concatenate([a_bd, r_bd], axis=0).astype(BF16)
        rhs = jnp.concatenate([b_bd, k_bd], axis=0).astype(BF16)
        gm = _dot_nt(lhs, rhs)
        n = 2 * c
        aab = jnp.where(past, gm[0:n, 0:n], 0.0)
        aak = jnp.where(past, gm[0:n, n:2 * n], 0.0)
        mrb = jnp.where(incl, gm[n:2 * n, 0:n], 0.0)
        mrk = jnp.where(incl, gm[n:2 * n, n:2 * n], 0.0)
        z = jnp.concatenate([a_bd, _dot(aak.astype(BF16), v_bd)], axis=1)
        xp = aab
        levels = int(math.log2(c))
        for lvl in range(levels):
            xb = xp.astype(BF16)
            z = z + _dot(xb, z.astype(BF16))
            if lvl + 1 < levels:
                xp = _dot(xb, xb)
        zb = z.astype(BF16)
        qz = _dot(mrb.astype(BF16), zb)
        q1_bd = r_bd + qz[:, 0:n]
        q2_bd = qz[:, n:2 * n] + _dot(mrk.astype(BF16), v_bd)
        bw_t = jnp.transpose(b_bd * wc).astype(BF16)
        kw_t = jnp.transpose(k_bd * wc).astype(BF16)
        gz = _dot(bw_t, zb)
        g1_bd = jnp.where(row2 == col2, wc, 0.0) + gz[:, 0:n]
        g2_bd = gz[:, n:2 * n] + _dot(kw_t, v_bd)
        g1_o[d, 0, 0] = _unstack_bd(g1_bd)
        g2_o[d, 0, 0] = _unstack_bd(g2_bd)
        q1_o[d, 0, 0] = _unstack_bd(q1_bd)
        q2_o[d, 0, 0] = _unstack_bd(q2_bd)


def _prep_call(r, v, kk, lwf, lwb, kmf, kmb, af, ab):
    b, t, w = r.shape
    n_pairs = w // LANES
    nch = t // CHUNK
    in_spec = pl.BlockSpec((1, CHUNK, LANES), lambda bi, p, j: (bi, j, p))
    out_spec = pl.BlockSpec((2, 1, 1, CHUNK, LANES), lambda bi, p, j: (0, j, bi * n_pairs + p, 0, 0))
    out_sd = jax.ShapeDtypeStruct((2, nch, b * n_pairs, CHUNK, LANES), F32)
    return pl.pallas_call(
        _prep_kernel,
        grid=(b, n_pairs, nch),
        in_specs=[in_spec] * 9,
        out_specs=[out_spec] * 4,
        out_shape=[out_sd] * 4,
        compiler_params=_cparams(("parallel", "parallel", "parallel")),
        name="prep",
    )(r, v, kk, lwf, lwb, kmf, kmb, af, ab)


def _scan_kernel(g1f, g2f, q1f, q2f, g1b, g2b, q1b, q2b, yf_o, yb_o, hf_s, hb_s):
    @pl.when(pl.program_id(0) == 0)
    def _():
        hf_s[...] = jnp.zeros_like(hf_s)
        hb_s[...] = jnp.zeros_like(hb_s)

    lane = lax.broadcasted_iota(jnp.int32, (CHUNK, LANES), 1)
    head0 = lane < HEAD_DIM
    n_pairs = hf_s.shape[0]
    for g1, g2, q1, q2, y_o, h_s in ((g1f, g2f, q1f, q2f, yf_o, hf_s), (g1b, g2b, q1b, q2b, yb_o, hb_s)):
        def body(p, carry, g1=g1, g2=g2, q1=q1, q2=q2, y_o=y_o, h_s=h_s):
            h = h_s[p]
            lhs = jnp.concatenate([_stack_bd(g1[0, 0, p], head0), q1[0, 0, p]], axis=0)
            res = _dot(lhs, h, HIGHEST)
            n = 2 * CHUNK
            h_s[p] = res[0:n] + _stack_bd(g2[0, 0, p], head0)
            y_o[0, p] = res[n:n + CHUNK] + q2[0, 0, p]
            return carry
        lax.fori_loop(0, n_pairs, body, 0)


def _scan_call(g1, g2, q1, q2, n_ctx_chunks):
    _, nch, npair, c, lanes = g1.shape
    n_lat_chunks = nch - n_ctx_chunks
    fwd = lambda s: (0, s, 0, 0, 0)
    bwd_chunk = lambda s: jnp.where(s < n_ctx_chunks, n_ctx_chunks - 1 - s, 2 * n_ctx_chunks + n_lat_chunks - 1 - s)
    bwd = lambda s: (1, bwd_chunk(s), 0, 0, 0)
    blk = (1, 1, npair, c, lanes)
    y_sd = jax.ShapeDtypeStruct((nch, npair, c, lanes), F32)
    return pl.pallas_call(
        _scan_kernel,
        grid=(nch,),
        in_specs=[pl.BlockSpec(blk, fwd)] * 4 + [pl.BlockSpec(blk, bwd)] * 4,
        out_specs=[pl.BlockSpec((1, npair, c, lanes), lambda s: (s, 0, 0, 0)),
                   pl.BlockSpec((1, npair, c, lanes), lambda s: (bwd_chunk(s), 0, 0, 0))],
        out_shape=[y_sd, y_sd],
        scratch_shapes=[pltpu.VMEM((npair, 2 * c, lanes), F32), pltpu.VMEM((npair, 2 * c, lanes), F32)],
        compiler_params=_cparams(("arbitrary",)),
        name="scan",
    )(g1, g2, q1, q2, g1, g2, q1, q2)


def _merge_kernel(yf_ref, yb_ref, bonus_ref, g_ref, zg_ref, yna_ref, x_ref, gt1_ref, sh2_ref, sc2_ref,
                  wpa_ref, wpr_ref, wo_ref, gnw_ref, gnb_ref, l1g_ref, l1b_ref, wrt_ref, ones_ref,
                  h1_o, u2_o, aff_o):
    tm = x_ref.shape[1]
    n_pairs = yf_ref.shape[1]
    ys = yf_ref[...] + yb_ref[...]
    y = jnp.concatenate([ys[:, p].reshape(tm, LANES) for p in range(n_pairs)], axis=1)
    ones_bd = ones_ref[...]
    inv = 1.0 / HEAD_DIM
    mu = _head_sum(y, ones_bd) * inv
    yc = y - mu
    var = _head_sum(yc * yc, ones_bd) * inv
    yn = yc * lax.rsqrt(var + GN_EPS) * gnw_ref[...] + gnb_ref[...]
    y_rw = (yn + bonus_ref[0]) * g_ref[0]
    gates = jax.nn.sigmoid(zg_ref[0].astype(F32))
    d = x_ref.shape[2]
    m1 = gates[:, 0:d] * _dot(yna_ref[0], wpa_ref[...]) + gates[:, d:2 * d] * _dot(y_rw.astype(BF16), wpr_ref[...])
    m = _dot(m1.astype(BF16), wo_ref[...])
    h1 = _ln(ALPHA * x_ref[0] + gt1_ref[0] * m, LN_EPS) * l1g_ref[...] + l1b_ref[...]
    h1_o[0] = h1
    u2 = _ln(h1, LN_EPS) * (1.0 + sc2_ref[0]) + sh2_ref[0]
    u2_o[0] = u2.astype(BF16)
    logits = _dot_nt(wrt_ref[...], u2, HIGHEST)
    mx = jnp.max(logits, axis=0, keepdims=True)
    e = jnp.exp(logits - mx)
    aff_o[0] = e / jnp.sum(e, axis=0, keepdims=True)


def _merge_call(yf, yb, bonus, g, zgate, y_na, x, gt1, sh2, sc2, w_pa, w_pr, w_o, gn_w, gn_b, l1g, l1b, w_rt,
                ones_bd, n_ctx):
    b, n, d = x.shape
    tm = ROW_TILE
    cpt = tm // CHUNK
    n_pairs = RW_WIDTH // LANES
    ctx_tiles = n_ctx // tm
    e = w_rt.shape[0]
    y_spec = pl.BlockSpec((cpt, n_pairs, CHUNK, LANES), lambda bi, i: (ctx_tiles + i, bi, 0, 0))
    full = lambda shape: pl.BlockSpec(shape, lambda bi, i: (0,) * len(shape))
    modv = pl.BlockSpec((1, 1, d), lambda bi, i: (bi, 0, 0))
    return pl.pallas_call(
        _merge_kernel,
        grid=(b, n // tm),
        in_specs=[y_spec, y_spec,
                  pl.BlockSpec((1, tm, RW_WIDTH), lambda bi, i: (bi, ctx_tiles + i, 0)),
                  pl.BlockSpec((1, tm, RW_WIDTH), lambda bi, i: (bi, ctx_tiles + i, 0)),
                  pl.BlockSpec((1, tm, 2 * d), lambda bi, i: (bi, ctx_tiles + i, 0)),
                  pl.BlockSpec((1, tm, NA_WIDTH), lambda bi, i: (bi, i, 0)),
                  pl.BlockSpec((1, tm, d), lambda bi, i: (bi, i, 0)),
                  modv, modv, modv,
                  full((NA_WIDTH, d)), full((RW_WIDTH, d)), full((d, d)),
                  full((1, RW_WIDTH)), full((1, RW_WIDTH)), full((1, d)), full((1, d)), full((e, d)),
                  full((RW_WIDTH, RW_WIDTH))],
        out_specs=[pl.BlockSpec((1, tm, d), lambda bi, i: (bi, i, 0)),
                   pl.BlockSpec((1, tm, d), lambda bi, i: (bi, i, 0)),
                   pl.BlockSpec((1, e, tm), lambda bi, i: (bi, 0, i))],
        out_shape=[jax.ShapeDtypeStruct((b, n, d), F32), jax.ShapeDtypeStruct((b, n, d), BF16),
                   jax.ShapeDtypeStruct((b, e, n), F32)],
        compiler_params=_cparams(("parallel", "parallel")),
        name="merge",
    )(yf, yb, bonus, g, zgate, y_na, x, gt1, sh2, sc2, w_pa, w_pr, w_o, gn_w, gn_b, l1g, l1b, w_rt, ones_bd)


def _ffn_kernel(x_ref, gate_ref, w1_ref, w3_ref, w2_ref, o_ref, acc_ref):
    f = pl.program_id(2)

    @pl.when(f == 0)
    def _():
        acc_ref[...] = jnp.zeros_like(acc_ref)

    xs = x_ref[0, 0]
    a = _dot(xs, w1_ref[0].astype(BF16))
    bb = _dot(xs, w3_ref[0].astype(BF16))
    hdn = (a * jax.nn.sigmoid(a) * bb).astype(BF16)
    acc_ref[...] += _dot(hdn, w2_ref[0].astype(BF16))

    @pl.when(f == pl.num_programs(2) - 1)
    def _():
        o_ref[0, 0] = acc_ref[...] * gate_ref[0, 0]


def _ffn_call(xs, gate, w1, w3, w2):
    b, e, c, d = xs.shape
    fdim = w1.shape[2]
    tf = FFN_TILE
    return pl.pallas_call(
        _ffn_kernel,
        grid=(e, b, fdim // tf),
        in_specs=[pl.BlockSpec((1, 1, c, d), lambda ei, bi, f: (bi, ei, 0, 0)),
                  pl.BlockSpec((1, 1, c, 1), lambda ei, bi, f: (bi, ei, 0, 0)),
                  pl.BlockSpec((1, d, tf), lambda ei, bi, f: (ei, 0, f)),
                  pl.BlockSpec((1, d, tf), lambda ei, bi, f: (ei, 0, f)),
                  pl.BlockSpec((1, tf, d), lambda ei, bi, f: (ei, f, 0))],
        out_specs=pl.BlockSpec((1, 1, c, d), lambda ei, bi, f: (bi, ei, 0, 0)),
        out_shape=jax.ShapeDtypeStruct((b, e, c, d), F32),
        scratch_shapes=[pltpu.VMEM((c, d), F32)],
        compiler_params=_cparams(("parallel", "parallel", "arbitrary")),
        name="ffn",
    )(xs, gate, w1, w3, w2)


def _final_kernel(h_ref, moe_ref, gt2_ref, g_ref, b_ref, o_ref):
    o_ref[0] = _ln(ALPHA * h_ref[0] + gt2_ref[0] * moe_ref[0], LN_EPS) * g_ref[...] + b_ref[...]


def _final_call(h1, moe, gt2, l2g, l2b):
    b, n, d = h1.shape
    tm = ROW_TILE
    row = pl.BlockSpec((1, tm, d), lambda bi, i: (bi, i, 0))
    vec = pl.BlockSpec((1, d), lambda bi, i: (0, 0))
    return pl.pallas_call(
        _final_kernel,
        grid=(b, n // tm),
        in_specs=[row, row, pl.BlockSpec((1, 1, d), lambda bi, i: (bi, 0, 0)), vec, vec],
        out_specs=row,
        out_shape=jax.ShapeDtypeStruct((b, n, d), F32),
        compiler_params=_cparams(("parallel", "parallel")),
        name="final",
    )(h1, moe, gt2, l2g, l2b)


def _rope_tables(n_ctx, n_lat):
    nf = HEAD_DIM // 4
    t = np.arange(n_lat)
    pos = np.stack([t // GRID_W, t % GRID_W], axis=-1).astype(np.float32)
    inv_freq = jnp.power(jnp.float32(ROPE_BASE), -jnp.arange(nf, dtype=F32) / nf)
    ang = jnp.asarray(pos)[:, :, None] * inv_freq
    cos = jnp.cos(ang)
    sin = jnp.sin(ang)
    cos_h = jnp.concatenate([cos[:, 0], cos[:, 0], cos[:, 1], cos[:, 1]], axis=-1)
    sin_h = jnp.concatenate([-sin[:, 0], sin[:, 0], -sin[:, 1], sin[:, 1]], axis=-1)
    cos_t = jnp.concatenate([jnp.ones((n_ctx, HEAD_DIM), F32), cos_h], axis=0)
    sin_t = jnp.concatenate([jnp.zeros((n_ctx, HEAD_DIM), F32), sin_h], axis=0)
    return jnp.tile(cos_t, (1, LANES // HEAD_DIM)), jnp.tile(sin_t, (1, LANES // HEAD_DIM))


def kernel(x, c, ctx, c_ctx, w_mod, b_mod, w_in, rpb, mu_prev, mu_next, w0, w_up, a0, a_up, g_up, k_k, k_a, r_k,
           gn_w, gn_b, w_pa, w_pr, w_o, ln1_g, ln1_b, w_router, w_e1, w_e3, w_e2, ln2_g, ln2_b):
    assert w_mod.shape[0] == DEPTH
    b, n, d = x.shape
    n_ctx = ctx.shape[1]
    assert n_ctx % ROW_TILE == 0 and n % ROW_TILE == 0 and (n // GRID_W) % NA_ROWS == 0
    t = n_ctx + n
    w = RW_WIDTH

    mod_rows = 8 * ((b + 1 + 7) // 8)
    cvec = jnp.zeros((mod_rows, d), F32).at[:b].set(c).at[b].set(c_ctx)
    mod = _mod_call(cvec, w_mod[0], b_mod[0][None])[:b + 1]
    sh1, sc1, gt1, sh2, sc2, gt2 = [m[:, None, :] for m in jnp.split(mod, 6, axis=-1)]

    xall = jnp.concatenate([ctx, x], axis=1)
    cos_tab, sin_tab = _rope_tables(n_ctx, n)
    q_all, k_all, v_all, zrw, zgate = _proj_call(xall, sh1, sc1, w_in[0].astype(BF16), cos_tab, sin_tab,
                                                 n_ctx // ROW_TILE)

    y_na = _natten_call(q_all, k_all, v_all, _natten_bias_table(rpb[0]), n_ctx, n)

    nt = t // ROW_TILE
    ctx_tiles = n_ctx // ROW_TILE
    zero_row = jnp.zeros((b, 1, RW_COLS), F32)
    prev_rows = jnp.concatenate([zero_row, zrw[:, ROW_TILE - 1:t - 1:ROW_TILE]], axis=1)
    next_rows = jnp.concatenate([zrw[:, ROW_TILE::ROW_TILE], zero_row], axis=1)
    seq_start = (np.arange(nt) == 0) | (np.arange(nt) == ctx_tiles)
    seq_end = (np.arange(nt) == ctx_tiles - 1) | (np.arange(nt) == nt - 1)
    prev_rows = jnp.where(jnp.asarray(seq_start)[None, :, None], 0.0, prev_rows)[:, :, None, :]
    next_rows = jnp.where(jnp.asarray(seq_end)[None, :, None], 0.0, next_rows)[:, :, None, :]
    zpad = jnp.zeros((DECAY_LORA, w), F32)
    wup_pad = jnp.stack([jnp.concatenate([w_up[0, 0], zpad], 0), jnp.concatenate([zpad, w_up[0, 1]], 0)])
    aup_pad = jnp.stack([jnp.concatenate([a_up[0, 0], zpad], 0), jnp.concatenate([zpad, a_up[0, 1]], 0)])
    ones_bd = jnp.asarray(np.kron(np.eye(w // HEAD_DIM), np.ones((HEAD_DIM, HEAD_DIM))), BF16)
    r, v, kk, lwf, lwb, kmf, kmb, af, ab, g, bonus = _feat_call(
        zrw, prev_rows, next_rows, mu_prev, mu_next, k_k, k_a, r_k[0].reshape(1, w), w0[0][:, None, :], wup_pad,
        a0[0][:, None, :], aup_pad.astype(BF16), g_up[0].astype(BF16), ones_bd)

    g1, g2, q1, q2 = _prep_call(r, v, kk, lwf, lwb, kmf, kmb, af, ab)
    yf, yb = _scan_call(g1, g2, q1, q2, n_ctx // CHUNK)

    h1, u2, aff_t = _merge_call(yf, yb, bonus, g, zgate, y_na, x, gt1[:b], sh2[:b], sc2[:b],
                                w_pa[0].astype(BF16), w_pr[0].astype(BF16), w_o[0].astype(BF16),
                                gn_w, gn_b, ln1_g, ln1_b, jnp.transpose(w_router[0]), ones_bd, n_ctx)

    cap = CAPACITY_FACTOR * n // N_EXPERTS
    gate, idx = lax.top_k(aff_t, cap)
    xs = jax.vmap(lambda ub, ib: ub[ib])(u2, idx)
    ye = _ffn_call(xs, gate[..., None], w_e1[0], w_e3[0], w_e2[0])
    moe = jax.vmap(lambda ib, yb_: jnp.zeros((n, d), F32).at[ib.reshape(-1)].add(yb_.reshape(-1, d)))(idx, ye)

    return _final_call(h1, moe, gt2[:b], ln2_g, ln2_b)
```

```python
import functools
import math

import jax
import jax.numpy as jnp
import numpy as np
from jax import lax
from jax.experimental import pallas as pl
from jax.experimental.pallas import tpu as pltpu

F32 = jnp.float32
BF16 = jnp.bfloat16
HIGHEST = lax.Precision.HIGHEST

GRID_W = 64
NA_HEADS = 8
HEAD_DIM = 64
NA_WIDTH = NA_HEADS * HEAD_DIM
WIN_H = 8
WIN_W = 16
ROPE_BASE = 10000.0
RW_WIDTH = 512
DECAY_LORA = 64
AAA_LORA = 64
GATE_LORA = 128
RW_COLS = 3 * RW_WIDTH + 2 * DECAY_LORA + 2 * AAA_LORA + GATE_LORA
N_EXPERTS = 16
CAPACITY_FACTOR = 2
DEPTH = 1
ALPHA = (2.0 * DEPTH) ** 0.25
LN_EPS = 1e-6
GN_EPS = 64e-5
NEG_BIG = -1e30

LANES = 128
CHUNK = 64
ROW_TILE = 256
NA_ROWS = 4
FFN_TILE = 256
VMEM_LIMIT = 48 * 1024 * 1024


def _cparams(sem):
    return pltpu.CompilerParams(dimension_semantics=sem, vmem_limit_bytes=VMEM_LIMIT)


def _dot(a, b, prec=None):
    return jnp.dot(a, b, preferred_element_type=F32, precision=prec)


def _dot_nt(a, b, prec=None):
    return lax.dot_general(a, b, (((1,), (1,)), ((), ())), preferred_element_type=F32, precision=prec)


def _dot_split3(a, b):
    a_hi = a.astype(BF16)
    a_lo = (a - a_hi.astype(F32)).astype(BF16)
    b_hi = b.astype(BF16)
    b_lo = (b - b_hi.astype(F32)).astype(BF16)
    m = a.shape[0]
    hh_lh = _dot(jnp.concatenate([a_hi, a_lo], axis=0), b_hi)
    return hh_lh[0:m] + hh_lh[m:2 * m] + _dot(a_hi, b_lo)


def _ln(x, eps):
    mu = jnp.mean(x, axis=-1, keepdims=True)
    xc = x - mu
    var = jnp.mean(xc * xc, axis=-1, keepdims=True)
    return xc * lax.rsqrt(var + eps)


def _head_sum(x, ones_bd):
    hi = x.astype(BF16)
    lo = (x - hi.astype(F32)).astype(BF16)
    return _dot(hi, ones_bd) + _dot(lo, ones_bd)


def _mod_kernel(c_ref, w_ref, b_ref, o_ref):
    cv = c_ref[...]
    s = cv * jax.nn.sigmoid(cv)
    o_ref[...] = _dot(s, w_ref[...], HIGHEST) + b_ref[...]


def _mod_call(cvec, w_mod, b_mod):
    rows, d = cvec.shape
    n = w_mod.shape[1]
    tn = 1536
    return pl.pallas_call(
        _mod_kernel,
        grid=(n // tn,),
        in_specs=[pl.BlockSpec((rows, d), lambda j: (0, 0)),
                  pl.BlockSpec((d, tn), lambda j: (0, j)),
                  pl.BlockSpec((1, tn), lambda j: (0, j))],
        out_specs=pl.BlockSpec((rows, tn), lambda j: (0, j)),
        out_shape=jax.ShapeDtypeStruct((rows, n), F32),
        compiler_params=_cparams(("parallel",)),
        name="mod",
    )(cvec, w_mod, b_mod)


def _rope(z, cos, sin, x1_lane):
    outs = []
    for j in range(z.shape[1] // LANES):
        zj = z[:, j * LANES:(j + 1) * LANES]
        partner = jnp.where(x1_lane, pltpu.roll(zj, LANES - 16, 1), pltpu.roll(zj, 16, 1))
        outs.append(zj * cos + partner * sin)
    return jnp.concatenate(outs, axis=1)


def _proj_kernel(x_ref, sh_ref, sc_ref, w_ref, cos_ref, sin_ref, q_ref, k_ref, v_ref, rw_ref, g_ref):
    x = x_ref[0]
    u = (_ln(x, LN_EPS) * (1.0 + sc_ref[0]) + sh_ref[0]).astype(BF16)
    cos = cos_ref[...]
    sin = sin_ref[...]
    lane = lax.broadcasted_iota(jnp.int32, cos.shape, 1)
    x1_lane = (lane % 32) < 16
    nw = NA_WIDTH
    zq = _dot(u, w_ref[:, 0:nw])
    q_ref[0] = (_rope(zq, cos, sin, x1_lane) * (HEAD_DIM ** -0.5)).astype(BF16)
    zk = _dot(u, w_ref[:, nw:2 * nw])
    k_ref[0] = _rope(zk, cos, sin, x1_lane).astype(BF16)
    v_ref[0] = _dot(u, w_ref[:, 2 * nw:3 * nw]).astype(BF16)
    rw_ref[0] = _dot(u, w_ref[:, 3 * nw:3 * nw + RW_COLS])
    g_ref[0] = _dot(u, w_ref[:, 3 * nw + RW_COLS:]).astype(BF16)


def _proj_call(xall, sh_tab, sc_tab, w_in_bf, cos_tab, sin_tab, n_ctx_tiles):
    b, t, d = xall.shape
    p_in = w_in_bf.shape[1]
    gate_cols = p_in - 3 * NA_WIDTH - RW_COLS
    tm = ROW_TILE
    mod_idx = lambda bi, i: (jnp.where(i < n_ctx_tiles, b, bi), 0, 0)
    row_blk = lambda w: pl.BlockSpec((1, tm, w), lambda bi, i: (bi, i, 0))
    return pl.pallas_call(
        _proj_kernel,
        grid=(b, t // tm),
        in_specs=[row_blk(d),
                  pl.BlockSpec((1, 1, d), mod_idx),
                  pl.BlockSpec((1, 1, d), mod_idx),
                  pl.BlockSpec((d, p_in), lambda bi, i: (0, 0)),
                  pl.BlockSpec((tm, LANES), lambda bi, i: (i, 0)),
                  pl.BlockSpec((tm, LANES), lambda bi, i: (i, 0))],
        out_specs=[row_blk(NA_WIDTH), row_blk(NA_WIDTH), row_blk(NA_WIDTH), row_blk(RW_COLS), row_blk(gate_cols)],
        out_shape=[jax.ShapeDtypeStruct((b, t, NA_WIDTH), BF16)] * 3
        + [jax.ShapeDtypeStruct((b, t, RW_COLS), F32), jax.ShapeDtypeStruct((b, t, gate_cols), BF16)],
        compiler_params=_cparams(("parallel", "parallel")),
        name="proj",
    )(xall, sh_tab, sc_tab, w_in_bf, cos_tab, sin_tab)


def _natten_kernel(q_ref, k_ref, v_ref, bias_ref, o_ref, *, n_ctx, n_rows):
    r = pl.program_id(2)
    kc = k_ref[0, 0:n_ctx, :]
    vc = v_ref[0, 0:n_ctx, :]
    lane2 = lax.broadcasted_iota(jnp.int32, (GRID_W, LANES), 1)
    head0 = lane2 < HEAD_DIM
    for ii in range(NA_ROWS):
        i = r * NA_ROWS + ii
        rs = jnp.clip(i - WIN_H // 2, 0, n_rows - WIN_H)
        off = rs - i + (WIN_H - 1)
        q = q_ref[0, ii * GRID_W:(ii + 1) * GRID_W, :]
        zero = jnp.zeros_like(q)
        q2 = jnp.concatenate([jnp.where(head0, q, zero), jnp.where(head0, zero, q)], axis=0)
        start = pl.multiple_of(n_ctx + rs * GRID_W, GRID_W)
        kw = k_ref[0, pl.ds(start, WIN_H * GRID_W), :]
        vw = v_ref[0, pl.ds(start, WIN_H * GRID_W), :]
        s_loc = _dot_nt(q2, kw) + bias_ref[0, off]
        s_ctx = _dot_nt(q2, kc)
        m = jnp.maximum(jnp.max(s_loc, axis=-1, keepdims=True), jnp.max(s_ctx, axis=-1, keepdims=True))
        p_loc = jnp.exp(s_loc - m)
        p_ctx = jnp.exp(s_ctx - m)
        denom = jnp.sum(p_loc, axis=-1, keepdims=True) + jnp.sum(p_ctx, axis=-1, keepdims=True)
        o2 = (_dot(p_loc.astype(BF16), vw) + _dot(p_ctx.astype(BF16), vc)) / denom
        out = jnp.where(head0, o2[0:GRID_W], o2[GRID_W:2 * GRID_W])
        o_ref[0, ii * GRID_W:(ii + 1) * GRID_W, :] = out.astype(o_ref.dtype)


def _natten_call(q_all, k_all, v_all, bias2, n_ctx, n_lat):
    b, t, _ = q_all.shape
    n_rows = n_lat // GRID_W
    blk = NA_ROWS * GRID_W
    ctx_blocks = n_ctx // blk
    n_pairs = NA_HEADS // 2
    kv_spec = pl.BlockSpec((1, t, LANES), lambda bi, hp, r: (bi, 0, hp))
    return pl.pallas_call(
        functools.partial(_natten_kernel, n_ctx=n_ctx, n_rows=n_rows),
        grid=(b, n_pairs, n_rows // NA_ROWS),
        in_specs=[pl.BlockSpec((1, blk, LANES), lambda bi, hp, r: (bi, ctx_blocks + r, hp)),
                  kv_spec, kv_spec,
                  pl.BlockSpec((1, WIN_H, 2 * GRID_W, WIN_H * GRID_W), lambda bi, hp, r: (hp, 0, 0, 0))],
        out_specs=pl.BlockSpec((1, blk, LANES), lambda bi, hp, r: (bi, r, hp)),
        out_shape=jax.ShapeDtypeStruct((b, n_lat, NA_WIDTH), BF16),
        compiler_params=_cparams(("parallel", "parallel", "arbitrary")),
        name="natten",
    )(q_all, k_all, v_all, bias2)


def _natten_bias_table(rpb):
    col = np.arange(GRID_W)
    col_start = np.clip(col - WIN_W // 2, 0, GRID_W - WIN_W)
    in_win = (col[None, :] >= col_start[:, None]) & (col[None, :] < col_start[:, None] + WIN_W)
    col_off = np.clip(col[None, :] - col[:, None] + (WIN_W - 1), 0, 2 * WIN_W - 2)
    row_off = np.arange(WIN_H)[:, None] + np.arange(WIN_H)[None, :]
    tab = rpb[:, row_off][:, :, :, col_off]
    tab = jnp.where(jnp.asarray(in_win)[None, None, None], tab, NEG_BIG)
    tab = jnp.transpose(tab, (0, 1, 3, 2, 4))
    h = rpb.shape[0]
    tab = tab.reshape(h // 2, 2, WIN_H, GRID_W, WIN_H * GRID_W)
    tab = jnp.transpose(tab, (0, 2, 1, 3, 4)).reshape(h // 2, WIN_H, 2 * GRID_W, WIN_H * GRID_W)
    return tab.astype(F32)


def _feat_kernel(z_ref, prev_ref, next_ref, mup_ref, mun_ref, kk_ref, ka_ref, rk_ref, w0_ref, wup_ref, a0_ref,
                 aup_ref, gup_ref, ones_ref,
                 r_o, v_o, kk_o, lwf_o, lwb_o, kmf_o, kmb_o, af_o, ab_o, g_o, bonus_o):
    z = z_ref[0]
    tm = z.shape[0]
    row = lax.broadcasted_iota(jnp.int32, (tm, 1), 0)
    zp = jnp.where(row == 0, prev_ref[0, 0], pltpu.roll(z, 1, 0))
    zn = jnp.where(row == tm - 1, next_ref[0, 0], pltpu.roll(z, tm - 1, 0))
    zs = z + mup_ref[...] * (zp - z) + mun_ref[...] * (zn - z)
    w = RW_WIDTH
    r = zs[:, 0:w]
    k = zs[:, w:2 * w]
    v = zs[:, 2 * w:3 * w]
    wd = jnp.tanh(zs[:, 3 * w:3 * w + 2 * DECAY_LORA])
    ad = zs[:, 3 * w + 2 * DECAY_LORA:3 * w + 2 * DECAY_LORA + 2 * AAA_LORA].astype(BF16)
    gd = jax.nn.sigmoid(zs[:, 3 * w + 2 * DECAY_LORA + 2 * AAA_LORA:]).astype(BF16)
    ones_bd = ones_ref[...]
    kkf = k * kk_ref[...]
    kk = kkf * lax.rsqrt(_head_sum(kkf * kkf, ones_bd) + 1e-12)
    r_o[0] = r
    v_o[0] = v
    kk_o[0] = kk
    g_o[0] = _dot(gd, gup_ref[...])
    rk = r * rk_ref[...]
    acc = jnp.zeros_like(r)
    for d, (lw_o, km_o, a_o) in enumerate(((lwf_o, kmf_o, af_o), (lwb_o, kmb_o, ab_o))):
        w_raw = w0_ref[d] + _dot(wd, wup_ref[d], HIGHEST)
        lw_o[0] = -math.exp(-0.5) * jax.nn.sigmoid(w_raw)
        a = jax.nn.sigmoid(a0_ref[d] + _dot(ad, aup_ref[d]))
        kmod = k * (1.0 + (a - 1.0) * ka_ref[...])
        a_o[0] = a
        km_o[0] = kmod
        acc = acc + rk * kmod
    bonus_o[0] = _head_sum(acc, ones_bd) * v


def _feat_call(zrw, prev_rows, next_rows, mu_prev, mu_next, k_k, k_a, r_k, w0, wup_pad, a0, aup_pad, g_up, ones_bd):
    b, t, c = zrw.shape
    tm = ROW_TILE
    w = RW_WIDTH
    full = lambda shape: pl.BlockSpec(shape, lambda bi, i: (0,) * len(shape))
    row = pl.BlockSpec((1, tm, w), lambda bi, i: (bi, i, 0))
    edge = pl.BlockSpec((1, 1, 1, c), lambda bi, i: (bi, i, 0, 0))
    return pl.pallas_call(
        _feat_kernel,
        grid=(b, t // tm),
        in_specs=[pl.BlockSpec((1, tm, c), lambda bi, i: (bi, i, 0)), edge, edge,
                  full((1, c)), full((1, c)), full((1, w)), full((1, w)), full((1, w)),
                  full((2, 1, w)), full((2, 2 * DECAY_LORA, w)), full((2, 1, w)), full((2, 2 * AAA_LORA, w)),
                  full((GATE_LORA, w)), full((w, w))],
        out_specs=[row] * 11,
        out_shape=[jax.ShapeDtypeStruct((b, t, w), F32)] * 11,
        compiler_params=_cparams(("parallel", "parallel")),
        name="feat",
    )(zrw, prev_rows, next_rows, mu_prev, mu_next, k_k, k_a, r_k, w0, wup_pad, a0, aup_pad, g_up, ones_bd)


def _stack_bd(x, head0):
    zero = jnp.zeros_like(x)
    return jnp.concatenate([jnp.where(head0, x, zero), jnp.where(head0, zero, x)], axis=0)


def _unstack_bd(x):
    c = x.shape[0] // 2
    return x[0:c] + x[c:2 * c]


def _prep_kernel(r_ref, v_ref, kk_ref, lwf_ref, lwb_ref, kmf_ref, kmb_ref, af_ref, ab_ref,
                 g1_o, g2_o, q1_o, q2_o):
    c = CHUNK
    n = 2 * c
    r = r_ref[0]
    v = v_ref[0]
    kk = kk_ref[0]
    n_pairs = r.shape[1] // LANES
    lane = lax.broadcasted_iota(jnp.int32, (c, LANES), 1)
    head0 = lane < HEAD_DIM
    ti = lax.broadcasted_iota(jnp.int32, (c, c), 0)
    si = lax.broadcasted_iota(jnp.int32, (c, c), 1)
    row2 = lax.broadcasted_iota(jnp.int32, (n, n), 0)
    col2 = lax.broadcasted_iota(jnp.int32, (n, n), 1)
    t2 = row2 % c
    s2 = col2 % c
    eye = row2 == col2
    levels = int(math.log2(c))
    pair = lambda x, p: x[:, p * LANES:(p + 1) * LANES]
    v_pairs = [_stack_bd(pair(v, p), head0).astype(BF16) for p in range(n_pairs)]
    wc, a_bd, r_bd, b_bd, k_bd, v_bd, past, incl, dest = [], [], [], [], [], [], [], [], []
    for d, (lw_ref, km_ref, a_ref) in enumerate(((lwf_ref, kmf_ref, af_ref), (lwb_ref, kmb_ref, ab_ref))):
        lw = lw_ref[0]
        if d == 0:
            tri = (si <= ti).astype(F32)
            past_d = s2 < t2
            last = c - 1
        else:
            tri = (si >= ti).astype(F32)
            past_d = s2 > t2
            last = 0
        incl_d = past_d | (s2 == t2)
        cs = _dot(tri, lw, HIGHEST)
        e_pos = jnp.exp(cs)
        e_neg = jnp.exp(-cs)
        a_all = -kk * jnp.exp(cs - lw)
        b_all = kk * a_ref[0] * e_neg
        k_all = km_ref[0] * e_neg
        r_all = r * e_pos
        wc_all = e_pos[last:last + 1, :]
        for p in range(n_pairs):
            wc.append(pair(wc_all, p))
            a_bd.append(_stack_bd(pair(a_all, p), head0))
            r_bd.append(_stack_bd(pair(r_all, p), head0))
            b_bd.append(_stack_bd(pair(b_all, p), head0))
            k_bd.append(_stack_bd(pair(k_all, p), head0))
            v_bd.append(v_pairs[p])
            past.append(past_d)
            incl.append(incl_d)
            dest.append((d, p))
    cs_ = range(len(dest))
    gm = [_dot_nt(jnp.concatenate([a_bd[i], r_bd[i]], axis=0).astype(BF16),
                  jnp.concatenate([b_bd[i], k_bd[i]], axis=0).astype(BF16)) for i in cs_]
    aab = [jnp.where(past[i], gm[i][0:n, 0:n], 0.0) for i in cs_]
    aak = [jnp.where(past[i], gm[i][0:n, n:2 * n], 0.0).astype(BF16) for i in cs_]
    mrb = [jnp.where(incl[i], gm[i][n:2 * n, 0:n], 0.0).astype(BF16) for i in cs_]
    mrk = [jnp.where(incl[i], gm[i][n:2 * n, n:2 * n], 0.0).astype(BF16) for i in cs_]
    akv = [_dot(aak[i], v_bd[i]) for i in cs_]
    tinv = [jnp.where(eye, 1.0, aab[i]) for i in cs_]
    xp = aab
    for _ in range(1, levels):
        xb = [xp[i].astype(BF16) for i in cs_]
        xp = [_dot(xb[i], xb[i]) for i in cs_]
        tinv = [tinv[i] + _dot(tinv[i].astype(BF16), xp[i].astype(BF16)) for i in cs_]
    zb = [_dot(tinv[i].astype(BF16), jnp.concatenate([a_bd[i], akv[i]], axis=1).astype(BF16)).astype(BF16)
          for i in cs_]
    qz = [_dot(mrb[i], zb[i]) for i in cs_]
    mv = [_dot(mrk[i], v_bd[i]) for i in cs_]
    bw_t = [jnp.transpose(b_bd[i] * wc[i]).astype(BF16) for i in cs_]
    kw_t = [jnp.transpose(k_bd[i] * wc[i]).astype(BF16) for i in cs_]
    gz = [_dot(bw_t[i], zb[i]) for i in cs_]
    kv = [_dot(kw_t[i], v_bd[i]) for i in cs_]
    for i, (d, p) in enumerate(dest):
        g1_o[d, 0, p] = _unstack_bd(jnp.where(eye, wc[i], 0.0) + gz[i][:, 0:n])
        g2_o[d, 0, p] = _unstack_bd(gz[i][:, n:2 * n] + kv[i])
        q1_o[d, 0, p] = _unstack_bd(r_bd[i] + qz[i][:, 0:n])
        q2_o[d, 0, p] = _unstack_bd(qz[i][:, n:2 * n] + mv[i])


def _prep_call(r, v, kk, lwf, lwb, kmf, kmb, af, ab):
    b, t, w = r.shape
    n_pairs = w // LANES
    nch = t // CHUNK
    in_spec = pl.BlockSpec((1, CHUNK, w), lambda bi, j: (bi, j, 0))
    out_spec = pl.BlockSpec((2, 1, n_pairs, CHUNK, LANES), lambda bi, j: (0, j, bi, 0, 0))
    out_sd = jax.ShapeDtypeStruct((2, nch, b * n_pairs, CHUNK, LANES), F32)
    return pl.pallas_call(
        _prep_kernel,
        grid=(b, nch),
        in_specs=[in_spec] * 9,
        out_specs=[out_spec] * 4,
        out_shape=[out_sd] * 4,
        compiler_params=_cparams(("parallel", "parallel")),
        name="prep",
    )(r, v, kk, lwf, lwb, kmf, kmb, af, ab)


def _scan_kernel(g1f, g2f, q1f, q2f, g1b, g2b, q1b, q2b, yf_o, yb_o, hf_s, hb_s):
    @pl.when(pl.program_id(0) == 0)
    def _():
        hf_s[...] = jnp.zeros_like(hf_s)
        hb_s[...] = jnp.zeros_like(hb_s)

    lane = lax.broadcasted_iota(jnp.int32, (CHUNK, LANES), 1)
    head0 = lane < HEAD_DIM
    n_pairs = hf_s.shape[0]
    n = 2 * CHUNK
    for g1, g2, q1, q2, y_o, h_s in ((g1f, g2f, q1f, q2f, yf_o, hf_s), (g1b, g2b, q1b, q2b, yb_o, hb_s)):
        for p in range(n_pairs):
            h = h_s[p]
            lhs = jnp.concatenate([_stack_bd(g1[0, 0, p], head0), q1[0, 0, p]], axis=0)
            res = _dot_split3(lhs, h)
            h_s[p] = res[0:n] + _stack_bd(g2[0, 0, p], head0)
            y_o[0, p] = res[n:n + CHUNK] + q2[0, 0, p]


def _scan_call(g1, g2, q1, q2, n_ctx_chunks):
    _, nch, npair, c, lanes = g1.shape
    n_lat_chunks = nch - n_ctx_chunks
    fwd = lambda s: (0, s, 0, 0, 0)
    bwd_chunk = lambda s: jnp.where(s < n_ctx_chunks, n_ctx_chunks - 1 - s, 2 * n_ctx_chunks + n_lat_chunks - 1 - s)
    bwd = lambda s: (1, bwd_chunk(s), 0, 0, 0)
    blk = (1, 1, npair, c, lanes)
    y_sd = jax.ShapeDtypeStruct((nch, npair, c, lanes), F32)
    return pl.pallas_call(
        _scan_kernel,
        grid=(nch,),
        in_specs=[pl.BlockSpec(blk, fwd)] * 4 + [pl.BlockSpec(blk, bwd)] * 4,
        out_specs=[pl.BlockSpec((1, npair, c, lanes), lambda s: (s, 0, 0, 0)),
                   pl.BlockSpec((1, npair, c, lanes), lambda s: (bwd_chunk(s), 0, 0, 0))],
        out_shape=[y_sd, y_sd],
        scratch_shapes=[pltpu.VMEM((npair, 2 * c, lanes), F32), pltpu.VMEM((npair, 2 * c, lanes), F32)],
        compiler_params=_cparams(("arbitrary",)),
        name="scan",
    )(g1, g2, q1, q2, g1, g2, q1, q2)


def _merge_kernel(yf_ref, yb_ref, bonus_ref, g_ref, zg_ref, yna_ref, x_ref, gt1_ref, sh2_ref, sc2_ref,
                  wpa_ref, wpr_ref, wo_ref, gnw_ref, gnb_ref, l1g_ref, l1b_ref, wrt_ref, ones_ref,
                  h1_o, u2_o, aff_o):
    tm = x_ref.shape[1]
    n_pairs = yf_ref.shape[1]
    ys = yf_ref[...] + yb_ref[...]
    y = jnp.concatenate([ys[:, p].reshape(tm, LANES) for p in range(n_pairs)], axis=1)
    ones_bd = ones_ref[...]
    inv = 1.0 / HEAD_DIM
    mu = _head_sum(y, ones_bd) * inv
    yc = y - mu
    var = _head_sum(yc * yc, ones_bd) * inv
    yn = yc * lax.rsqrt(var + GN_EPS) * gnw_ref[...] + gnb_ref[...]
    y_rw = (yn + bonus_ref[0]) * g_ref[0]
    gates = jax.nn.sigmoid(zg_ref[0].astype(F32))
    d = x_ref.shape[2]
    m1 = gates[:, 0:d] * _dot(yna_ref[0], wpa_ref[...]) + gates[:, d:2 * d] * _dot(y_rw.astype(BF16), wpr_ref[...])
    m = _dot(m1.astype(BF16), wo_ref[...])
    h1 = _ln(ALPHA * x_ref[0] + gt1_ref[0] * m, LN_EPS) * l1g_ref[...] + l1b_ref[...]
    h1_o[0] = h1
    u2 = _ln(h1, LN_EPS) * (1.0 + sc2_ref[0]) + sh2_ref[0]
    u2_o[0] = u2.astype(BF16)
    logits = _dot_nt(wrt_ref[...], u2, HIGHEST)
    mx = jnp.max(logits, axis=0, keepdims=True)
    e = jnp.exp(logits - mx)
    aff_o[0] = e / jnp.sum(e, axis=0, keepdims=True)


def _merge_call(yf, yb, bonus, g, zgate, y_na, x, gt1, sh2, sc2, w_pa, w_pr, w_o, gn_w, gn_b, l1g, l1b, w_rt,
                ones_bd, n_ctx):
    b, n, d = x.shape
    tm = ROW_TILE
    cpt = tm // CHUNK
    n_pairs = RW_WIDTH // LANES
    ctx_tiles = n_ctx // tm
    e = w_rt.shape[0]
    y_spec = pl.BlockSpec((cpt, n_pairs, CHUNK, LANES), lambda bi, i: (ctx_tiles + i, bi, 0, 0))
    full = lambda shape: pl.BlockSpec(shape, lambda bi, i: (0,) * len(shape))
    modv = pl.BlockSpec((1, 1, d), lambda bi, i: (bi, 0, 0))
    return pl.pallas_call(
        _merge_kernel,
        grid=(b, n // tm),
        in_specs=[y_spec, y_spec,
                  pl.BlockSpec((1, tm, RW_WIDTH), lambda bi, i: (bi, ctx_tiles + i, 0)),
                  pl.BlockSpec((1, tm, RW_WIDTH), lambda bi, i: (bi, ctx_tiles + i, 0)),
                  pl.BlockSpec((1, tm, 2 * d), lambda bi, i: (bi, ctx_tiles + i, 0)),
                  pl.BlockSpec((1, tm, NA_WIDTH), lambda bi, i: (bi, i, 0)),
                  pl.BlockSpec((1, tm, d), lambda bi, i: (bi, i, 0)),
                  modv, modv, modv,
                  full((NA_WIDTH, d)), full((RW_WIDTH, d)), full((d, d)),
                  full((1, RW_WIDTH)), full((1, RW_WIDTH)), full((1, d)), full((1, d)), full((e, d)),
                  full((RW_WIDTH, RW_WIDTH))],
        out_specs=[pl.BlockSpec((1, tm, d), lambda bi, i: (bi, i, 0)),
                   pl.BlockSpec((1, tm, d), lambda bi, i: (bi, i, 0)),
                   pl.BlockSpec((1, e, tm), lambda bi, i: (bi, 0, i))],
        out_shape=[jax.ShapeDtypeStruct((b, n, d), F32), jax.ShapeDtypeStruct((b, n, d), BF16),
                   jax.ShapeDtypeStruct((b, e, n), F32)],
        compiler_params=_cparams(("parallel", "parallel")),
        name="merge",
    )(yf, yb, bonus, g, zgate, y_na, x, gt1, sh2, sc2, w_pa, w_pr, w_o, gn_w, gn_b, l1g, l1b, w_rt, ones_bd)


def _ffn_kernel(x_ref, gate_ref, w1_ref, w3_ref, w2_ref, o_ref, acc_ref):
    f = pl.program_id(2)

    @pl.when(f == 0)
    def _():
        acc_ref[...] = jnp.zeros_like(acc_ref)

    xs = x_ref[0, 0]
    a = _dot(xs, w1_ref[0].astype(BF16))
    bb = _dot(xs, w3_ref[0].astype(BF16))
    hdn = (a * jax.nn.sigmoid(a) * bb).astype(BF16)
    acc_ref[...] += _dot(hdn, w2_ref[0].astype(BF16))

    @pl.when(f == pl.num_programs(2) - 1)
    def _():
        o_ref[0, 0] = acc_ref[...] * gate_ref[0, 0]


def _ffn_call(xs, gate, w1, w3, w2):
    b, e, c, d = xs.shape
    fdim = w1.shape[2]
    tf = FFN_TILE
    return pl.pallas_call(
        _ffn_kernel,
        grid=(e, b, fdim // tf),
        in_specs=[pl.BlockSpec((1, 1, c, d), lambda ei, bi, f: (bi, ei, 0, 0)),
                  pl.BlockSpec((1, 1, c, 1), lambda ei, bi, f: (bi, ei, 0, 0)),
                  pl.BlockSpec((1, d, tf), lambda ei, bi, f: (ei, 0, f)),
                  pl.BlockSpec((1, d, tf), lambda ei, bi, f: (ei, 0, f)),
                  pl.BlockSpec((1, tf, d), lambda ei, bi, f: (ei, f, 0))],
        out_specs=pl.BlockSpec((1, 1, c, d), lambda ei, bi, f: (bi, ei, 0, 0)),
        out_shape=jax.ShapeDtypeStruct((b, e, c, d), F32),
        scratch_shapes=[pltpu.VMEM((c, d), F32)],
        compiler_params=_cparams(("parallel", "parallel", "arbitrary")),
        name="ffn",
    )(xs, gate, w1, w3, w2)


def _final_kernel(h_ref, moe_ref, gt2_ref, g_ref, b_ref, o_ref):
    o_ref[0] = _ln(ALPHA * h_ref[0] + gt2_ref[0] * moe_ref[0], LN_EPS) * g_ref[...] + b_ref[...]


def _final_call(h1, moe, gt2, l2g, l2b):
    b, n, d = h1.shape
    tm = ROW_TILE
    row = pl.BlockSpec((1, tm, d), lambda bi, i: (bi, i, 0))
    vec = pl.BlockSpec((1, d), lambda bi, i: (0, 0))
    return pl.pallas_call(
        _final_kernel,
        grid=(b, n // tm),
        in_specs=[row, row, pl.BlockSpec((1, 1, d), lambda bi, i: (bi, 0, 0)), vec, vec],
        out_specs=row,
        out_shape=jax.ShapeDtypeStruct((b, n, d), F32),
        compiler_params=_cparams(("parallel", "parallel")),
        name="final",
    )(h1, moe, gt2, l2g, l2b)


def _rope_tables(n_ctx, n_lat):
    nf = HEAD_DIM // 4
    t = np.arange(n_lat)
    pos = np.stack([t // GRID_W, t % GRID_W], axis=-1).astype(np.float32)
    inv_freq = jnp.power(jnp.float32(ROPE_BASE), -jnp.arange(nf, dtype=F32) / nf)
    ang = jnp.asarray(pos)[:, :, None] * inv_freq
    cos = jnp.cos(ang)
    sin = jnp.sin(ang)
    cos_h = jnp.concatenate([cos[:, 0], cos[:, 0], cos[:, 1], cos[:, 1]], axis=-1)
    sin_h = jnp.concatenate([-sin[:, 0], sin[:, 0], -sin[:, 1], sin[:, 1]], axis=-1)
    cos_t = jnp.concatenate([jnp.ones((n_ctx, HEAD_DIM), F32), cos_h], axis=0)
    sin_t = jnp.concatenate([jnp.zeros((n_ctx, HEAD_DIM), F32), sin_h], axis=0)
    return jnp.tile(cos_t, (1, LANES // HEAD_DIM)), jnp.tile(sin_t, (1, LANES // HEAD_DIM))


def kernel(x, c, ctx, c_ctx, w_mod, b_mod, w_in, rpb, mu_prev, mu_next, w0, w_up, a0, a_up, g_up, k_k, k_a, r_k,
           gn_w, gn_b, w_pa, w_pr, w_o, ln1_g, ln1_b, w_router, w_e1, w_e3, w_e2, ln2_g, ln2_b):
    assert w_mod.shape[0] == DEPTH
    b, n, d = x.shape
    n_ctx = ctx.shape[1]
    assert n_ctx % ROW_TILE == 0 and n % ROW_TILE == 0 and (n // GRID_W) % NA_ROWS == 0
    t = n_ctx + n
    w = RW_WIDTH

    mod_rows = 8 * ((b + 1 + 7) // 8)
    cvec = jnp.zeros((mod_rows, d), F32).at[:b].set(c).at[b].set(c_ctx)
    mod = _mod_call(cvec, w_mod[0], b_mod[0][None])[:b + 1]
    sh1, sc1, gt1, sh2, sc2, gt2 = [mod[:, None, i * d:(i + 1) * d] for i in range(6)]

    xall = jnp.concatenate([ctx, x], axis=1)
    cos_tab, sin_tab = _rope_tables(n_ctx, n)
    q_all, k_all, v_all, zrw, zgate = _proj_call(xall, sh1, sc1, w_in[0].astype(BF16), cos_tab, sin_tab,
                                                 n_ctx // ROW_TILE)

    y_na = _natten_call(q_all, k_all, v_all, _natten_bias_table(rpb[0]), n_ctx, n)

    nt = t // ROW_TILE
    ctx_tiles = n_ctx // ROW_TILE
    zero_row = jnp.zeros((b, 1, RW_COLS), F32)
    prev_rows = jnp.concatenate([zero_row, zrw[:, ROW_TILE - 1:t - 1:ROW_TILE]], axis=1)
    next_rows = jnp.concatenate([zrw[:, ROW_TILE::ROW_TILE], zero_row], axis=1)
    seq_start = (np.arange(nt) == 0) | (np.arange(nt) == ctx_tiles)
    seq_end = (np.arange(nt) == ctx_tiles - 1) | (np.arange(nt) == nt - 1)
    prev_rows = jnp.where(jnp.asarray(seq_start)[None, :, None], 0.0, prev_rows)[:, :, None, :]
    next_rows = jnp.where(jnp.asarray(seq_end)[None, :, None], 0.0, next_rows)[:, :, None, :]
    zpad = jnp.zeros((DECAY_LORA, w), F32)
    wup_pad = jnp.stack([jnp.concatenate([w_up[0, 0], zpad], 0), jnp.concatenate([zpad, w_up[0, 1]], 0)])
    aup_pad = jnp.stack([jnp.concatenate([a_up[0, 0], zpad], 0), jnp.concatenate([zpad, a_up[0, 1]], 0)])
    ones_bd = jnp.asarray(np.kron(np.eye(w // HEAD_DIM), np.ones((HEAD_DIM, HEAD_DIM))), BF16)
    r, v, kk, lwf, lwb, kmf, kmb, af, ab, g, bonus = _feat_call(
        zrw, prev_rows, next_rows, mu_prev, mu_next, k_k, k_a, r_k[0].reshape(1, w), w0[0][:, None, :], wup_pad,
        a0[0][:, None, :], aup_pad.astype(BF16), g_up[0].astype(BF16), ones_bd)

    g1, g2, q1, q2 = _prep_call(r, v, kk, lwf, lwb, kmf, kmb, af, ab)
    yf, yb = _scan_call(g1, g2, q1, q2, n_ctx // CHUNK)

    h1, u2, aff_t = _merge_call(yf, yb, bonus, g, zgate, y_na, x, gt1[:b], sh2[:b], sc2[:b],
                                w_pa[0].astype(BF16), w_pr[0].astype(BF16), w_o[0].astype(BF16),
                                gn_w, gn_b, ln1_g, ln1_b, jnp.transpose(w_router[0]), ones_bd, n_ctx)

    cap = CAPACITY_FACTOR * n // N_EXPERTS
    gate, idx = lax.top_k(aff_t, cap)
    flat_idx = (idx + (jnp.arange(b, dtype=jnp.int32) * n)[:, None, None]).reshape(-1)
    xs = jnp.take(u2.reshape(b * n, d), flat_idx, axis=0).reshape(b, N_EXPERTS, cap, d)
    ye = _ffn_call(xs, gate[..., None], w_e1[0], w_e3[0], w_e2[0])
    moe = jnp.zeros((b * n, d), F32).at[flat_idx].add(ye.reshape(-1, d)).reshape(b, n, d)

    return _final_call(h1, moe, gt2[:b], ln2_g, ln2_b)
```

```python
import functools
import math

import jax
import jax.numpy as jnp
import numpy as np
from jax import lax
from jax.experimental import pallas as pl
from jax.experimental.pallas import tpu as pltpu

F32 = jnp.float32
BF16 = jnp.bfloat16
HIGHEST = lax.Precision.HIGHEST

GRID_W = 64
NA_HEADS = 8
HEAD_DIM = 64
NA_WIDTH = NA_HEADS * HEAD_DIM
WIN_H = 8
WIN_W = 16
ROPE_BASE = 10000.0
RW_WIDTH = 512
DECAY_LORA = 64
AAA_LORA = 64
GATE_LORA = 128
RW_COLS = 3 * RW_WIDTH + 2 * DECAY_LORA + 2 * AAA_LORA + GATE_LORA
N_EXPERTS = 16
CAPACITY_FACTOR = 2
DEPTH = 1
ALPHA = (2.0 * DEPTH) ** 0.25
LN_EPS = 1e-6
GN_EPS = 64e-5
NEG_BIG = -1e30

LANES = 128
CHUNK = 64
ROW_TILE = 256
NA_ROWS = 4
FFN_TILE = 256
FFN_OUT_TILE = 512
VMEM_LIMIT = 48 * 1024 * 1024


def _cparams(sem):
    return pltpu.CompilerParams(dimension_semantics=sem, vmem_limit_bytes=VMEM_LIMIT)


def _dot(a, b, prec=None):
    return jnp.dot(a, b, preferred_element_type=F32, precision=prec)


def _dot_nt(a, b, prec=None):
    return lax.dot_general(a, b, (((1,), (1,)), ((), ())), preferred_element_type=F32, precision=prec)


def _dot_split3(a, b):
    a_hi = a.astype(BF16)
    a_lo = (a - a_hi.astype(F32)).astype(BF16)
    b_hi = b.astype(BF16)
    b_lo = (b - b_hi.astype(F32)).astype(BF16)
    m = a.shape[0]
    hh_lh = _dot(jnp.concatenate([a_hi, a_lo], axis=0), b_hi)
    return hh_lh[0:m] + hh_lh[m:2 * m] + _dot(a_hi, b_lo)


def _ln(x, eps):
    mu = jnp.mean(x, axis=-1, keepdims=True)
    xc = x - mu
    var = jnp.mean(xc * xc, axis=-1, keepdims=True)
    return xc * lax.rsqrt(var + eps)


def _head_sum(x, ones_bd):
    hi = x.astype(BF16)
    lo = (x - hi.astype(F32)).astype(BF16)
    return _dot(hi, ones_bd) + _dot(lo, ones_bd)


def _mod_kernel(c_ref, w_ref, b_ref, o_ref):
    cv = c_ref[...]
    s = cv * jax.nn.sigmoid(cv)
    o_ref[...] = _dot(s, w_ref[...], HIGHEST) + b_ref[...]


def _mod_call(cvec, w_mod, b_mod):
    rows, d = cvec.shape
    n = w_mod.shape[1]
    tn = 1536
    return pl.pallas_call(
        _mod_kernel,
        grid=(n // tn,),
        in_specs=[pl.BlockSpec((rows, d), lambda j: (0, 0)),
                  pl.BlockSpec((d, tn), lambda j: (0, j)),
                  pl.BlockSpec((1, tn), lambda j: (0, j))],
        out_specs=pl.BlockSpec((rows, tn), lambda j: (0, j)),
        out_shape=jax.ShapeDtypeStruct((rows, n), F32),
        compiler_params=_cparams(("parallel",)),
        name="mod",
    )(cvec, w_mod, b_mod)


def _rope(z, cos, sin, x1_lane):
    outs = []
    for j in range(z.shape[1] // LANES):
        zj = z[:, j * LANES:(j + 1) * LANES]
        partner = jnp.where(x1_lane, pltpu.roll(zj, LANES - 16, 1), pltpu.roll(zj, 16, 1))
        outs.append(zj * cos + partner * sin)
    return jnp.concatenate(outs, axis=1)


def _proj_kernel(ctx_ref, x_ref, sh_ref, sc_ref, w_ref, cos_ref, sin_ref,
                 q_ref, k_ref, v_ref, rw_ref, g_ref, first_ref, last_ref, *, n_ctx_tiles):
    x = jnp.where(pl.program_id(1) < n_ctx_tiles, ctx_ref[0], x_ref[0])
    u = (_ln(x, LN_EPS) * (1.0 + sc_ref[0]) + sh_ref[0]).astype(BF16)
    cos = cos_ref[...]
    sin = sin_ref[...]
    lane = lax.broadcasted_iota(jnp.int32, cos.shape, 1)
    x1_lane = (lane % 32) < 16
    nw = NA_WIDTH
    zq = _dot(u, w_ref[:, 0:nw])
    q_ref[0] = (_rope(zq, cos, sin, x1_lane) * (HEAD_DIM ** -0.5)).astype(BF16)
    zk = _dot(u, w_ref[:, nw:2 * nw])
    k_ref[0] = _rope(zk, cos, sin, x1_lane).astype(BF16)
    v_ref[0] = _dot(u, w_ref[:, 2 * nw:3 * nw]).astype(BF16)
    zrw = _dot(u, w_ref[:, 3 * nw:3 * nw + RW_COLS])
    rw_ref[0] = zrw
    first_ref[0, 0] = zrw[0:1]
    last_ref[0, 0] = zrw[zrw.shape[0] - 1:]
    g_ref[0] = _dot(u, w_ref[:, 3 * nw + RW_COLS:]).astype(BF16)


def _proj_call(ctx, x, sh_tab, sc_tab, w_in_bf, cos_tab, sin_tab):
    b, n_ctx, d = ctx.shape
    t = n_ctx + x.shape[1]
    p_in = w_in_bf.shape[1]
    gate_cols = p_in - 3 * NA_WIDTH - RW_COLS
    tm = ROW_TILE
    n_ctx_tiles = n_ctx // tm
    nt = t // tm
    mod_idx = lambda bi, i: (jnp.where(i < n_ctx_tiles, b, bi), 0, 0)
    row_blk = lambda w: pl.BlockSpec((1, tm, w), lambda bi, i: (bi, i, 0))
    edge_blk = pl.BlockSpec((1, 1, 1, RW_COLS), lambda bi, i: (bi, i, 0, 0))
    edge_sd = jax.ShapeDtypeStruct((b, nt, 1, RW_COLS), F32)
    return pl.pallas_call(
        functools.partial(_proj_kernel, n_ctx_tiles=n_ctx_tiles),
        grid=(b, nt),
        in_specs=[pl.BlockSpec((1, tm, d), lambda bi, i: (bi, jnp.minimum(i, n_ctx_tiles - 1), 0)),
                  pl.BlockSpec((1, tm, d), lambda bi, i: (bi, jnp.maximum(i - n_ctx_tiles, 0), 0)),
                  pl.BlockSpec((1, 1, d), mod_idx),
                  pl.BlockSpec((1, 1, d), mod_idx),
                  pl.BlockSpec((d, p_in), lambda bi, i: (0, 0)),
                  pl.BlockSpec((tm, LANES), lambda bi, i: (i, 0)),
                  pl.BlockSpec((tm, LANES), lambda bi, i: (i, 0))],
        out_specs=[row_blk(NA_WIDTH), row_blk(NA_WIDTH), row_blk(NA_WIDTH), row_blk(RW_COLS), row_blk(gate_cols),
                   edge_blk, edge_blk],
        out_shape=[jax.ShapeDtypeStruct((b, t, NA_WIDTH), BF16)] * 3
        + [jax.ShapeDtypeStruct((b, t, RW_COLS), F32), jax.ShapeDtypeStruct((b, t, gate_cols), BF16),
           edge_sd, edge_sd],
        compiler_params=_cparams(("parallel", "parallel")),
        name="proj",
    )(ctx, x, sh_tab, sc_tab, w_in_bf, cos_tab, sin_tab)


def _natten_kernel(q_ref, k_ref, v_ref, bias_ref, o_ref, *, n_ctx, n_rows):
    r = pl.program_id(2)
    kc = k_ref[0, 0:n_ctx, :]
    vc = v_ref[0, 0:n_ctx, :]
    lane2 = lax.broadcasted_iota(jnp.int32, (GRID_W, LANES), 1)
    head0 = lane2 < HEAD_DIM
    for ii in range(NA_ROWS):
        i = r * NA_ROWS + ii
        rs = jnp.clip(i - WIN_H // 2, 0, n_rows - WIN_H)
        off = rs - i + (WIN_H - 1)
        q = q_ref[0, ii * GRID_W:(ii + 1) * GRID_W, :]
        zero = jnp.zeros_like(q)
        q2 = jnp.concatenate([jnp.where(head0, q, zero), jnp.where(head0, zero, q)], axis=0)
        start = pl.multiple_of(n_ctx + rs * GRID_W, GRID_W)
        kw = k_ref[0, pl.ds(start, WIN_H * GRID_W), :]
        vw = v_ref[0, pl.ds(start, WIN_H * GRID_W), :]
        s_loc = _dot_nt(q2, kw) + bias_ref[0, off]
        s_ctx = _dot_nt(q2, kc)
        m = jnp.maximum(jnp.max(s_loc, axis=-1, keepdims=True), jnp.max(s_ctx, axis=-1, keepdims=True))
        p_loc = jnp.exp(s_loc - m)
        p_ctx = jnp.exp(s_ctx - m)
        denom = jnp.sum(p_loc, axis=-1, keepdims=True) + jnp.sum(p_ctx, axis=-1, keepdims=True)
        o2 = (_dot(p_loc.astype(BF16), vw) + _dot(p_ctx.astype(BF16), vc)) / denom
        out = jnp.where(head0, o2[0:GRID_W], o2[GRID_W:2 * GRID_W])
        o_ref[0, ii * GRID_W:(ii + 1) * GRID_W, :] = out.astype(o_ref.dtype)


def _natten_call(q_all, k_all, v_all, bias2, n_ctx, n_lat):
    b, t, _ = q_all.shape
    n_rows = n_lat // GRID_W
    blk = NA_ROWS * GRID_W
    ctx_blocks = n_ctx // blk
    n_pairs = NA_HEADS // 2
    kv_spec = pl.BlockSpec((1, t, LANES), lambda bi, hp, r: (bi, 0, hp))
    return pl.pallas_call(
        functools.partial(_natten_kernel, n_ctx=n_ctx, n_rows=n_rows),
        grid=(b, n_pairs, n_rows // NA_ROWS),
        in_specs=[pl.BlockSpec((1, blk, LANES), lambda bi, hp, r: (bi, ctx_blocks + r, hp)),
                  kv_spec, kv_spec,
                  pl.BlockSpec((1, WIN_H, 2 * GRID_W, WIN_H * GRID_W), lambda bi, hp, r: (hp, 0, 0, 0))],
        out_specs=pl.BlockSpec((1, blk, LANES), lambda bi, hp, r: (bi, r, hp)),
        out_shape=jax.ShapeDtypeStruct((b, n_lat, NA_WIDTH), BF16),
        compiler_params=_cparams(("parallel", "parallel", "arbitrary")),
        name="natten",
    )(q_all, k_all, v_all, bias2)


def _natten_bias_table(rpb):
    col = np.arange(GRID_W)
    col_start = np.clip(col - WIN_W // 2, 0, GRID_W - WIN_W)
    in_win = (col[None, :] >= col_start[:, None]) & (col[None, :] < col_start[:, None] + WIN_W)
    col_off = np.clip(col[None, :] - col[:, None] + (WIN_W - 1), 0, 2 * WIN_W - 2)
    row_off = np.arange(WIN_H)[:, None] + np.arange(WIN_H)[None, :]
    tab = rpb[:, row_off][:, :, :, col_off]
    tab = jnp.where(jnp.asarray(in_win)[None, None, None], tab, NEG_BIG)
    tab = jnp.transpose(tab, (0, 1, 3, 2, 4))
    h = rpb.shape[0]
    tab = tab.reshape(h // 2, 2, WIN_H, GRID_W, WIN_H * GRID_W)
    tab = jnp.transpose(tab, (0, 2, 1, 3, 4)).reshape(h // 2, WIN_H, 2 * GRID_W, WIN_H * GRID_W)
    return tab.astype(F32)


def _feat_kernel(z_ref, prev_ref, next_ref, mup_ref, mun_ref, kk_ref, ka_ref, rk_ref, w0_ref, wup_ref, a0_ref,
                 aup_ref, gup_ref, ones_ref,
                 r_o, v_o, kk_o, lwf_o, lwb_o, kmf_o, kmb_o, af_o, ab_o, g_o, bonus_o):
    z = z_ref[0]
    tm = z.shape[0]
    row = lax.broadcasted_iota(jnp.int32, (tm, 1), 0)
    zp = jnp.where(row == 0, prev_ref[0, 0], pltpu.roll(z, 1, 0))
    zn = jnp.where(row == tm - 1, next_ref[0, 0], pltpu.roll(z, tm - 1, 0))
    zs = z + mup_ref[...] * (zp - z) + mun_ref[...] * (zn - z)
    w = RW_WIDTH
    r = zs[:, 0:w]
    k = zs[:, w:2 * w]
    v = zs[:, 2 * w:3 * w]
    wd = jnp.tanh(zs[:, 3 * w:3 * w + 2 * DECAY_LORA])
    ad = zs[:, 3 * w + 2 * DECAY_LORA:3 * w + 2 * DECAY_LORA + 2 * AAA_LORA].astype(BF16)
    gd = jax.nn.sigmoid(zs[:, 3 * w + 2 * DECAY_LORA + 2 * AAA_LORA:]).astype(BF16)
    ones_bd = ones_ref[...]
    kkf = k * kk_ref[...]
    kk = kkf * lax.rsqrt(_head_sum(kkf * kkf, ones_bd) + 1e-12)
    r_o[0] = r
    v_o[0] = v
    kk_o[0] = kk
    g_o[0] = _dot(gd, gup_ref[...])
    rk = r * rk_ref[...]
    acc = jnp.zeros_like(r)
    for d, (lw_o, km_o, a_o) in enumerate(((lwf_o, kmf_o, af_o), (lwb_o, kmb_o, ab_o))):
        w_raw = w0_ref[d] + _dot(wd, wup_ref[d], HIGHEST)
        lw_o[0] = -math.exp(-0.5) * jax.nn.sigmoid(w_raw)
        a = jax.nn.sigmoid(a0_ref[d] + _dot(ad, aup_ref[d]))
        kmod = k * (1.0 + (a - 1.0) * ka_ref[...])
        a_o[0] = a
        km_o[0] = kmod
        acc = acc + rk * kmod
    bonus_o[0] = _head_sum(acc, ones_bd) * v


def _feat_call(zrw, prev_rows, next_rows, mu_prev, mu_next, k_k, k_a, r_k, w0, wup_pad, a0, aup_pad, g_up, ones_bd):
    b, t, c = zrw.shape
    tm = ROW_TILE
    w = RW_WIDTH
    full = lambda shape: pl.BlockSpec(shape, lambda bi, i: (0,) * len(shape))
    row = pl.BlockSpec((1, tm, w), lambda bi, i: (bi, i, 0))
    edge = pl.BlockSpec((1, 1, 1, c), lambda bi, i: (bi, i, 0, 0))
    return pl.pallas_call(
        _feat_kernel,
        grid=(b, t // tm),
        in_specs=[pl.BlockSpec((1, tm, c), lambda bi, i: (bi, i, 0)), edge, edge,
                  full((1, c)), full((1, c)), full((1, w)), full((1, w)), full((1, w)),
                  full((2, 1, w)), full((2, 2 * DECAY_LORA, w)), full((2, 1, w)), full((2, 2 * AAA_LORA, w)),
                  full((GATE_LORA, w)), full((w, w))],
        out_specs=[row] * 11,
        out_shape=[jax.ShapeDtypeStruct((b, t, w), F32)] * 11,
        compiler_params=_cparams(("parallel", "parallel")),
        name="feat",
    )(zrw, prev_rows, next_rows, mu_prev, mu_next, k_k, k_a, r_k, w0, wup_pad, a0, aup_pad, g_up, ones_bd)


def _stack_bd(x, head0):
    zero = jnp.zeros_like(x)
    return jnp.concatenate([jnp.where(head0, x, zero), jnp.where(head0, zero, x)], axis=0)


def _unstack_bd(x):
    c = x.shape[0] // 2
    return x[0:c] + x[c:2 * c]


def _prep_kernel(r_ref, v_ref, kk_ref, lwf_ref, lwb_ref, kmf_ref, kmb_ref, af_ref, ab_ref,
                 g1_o, g2_o, q1_o, q2_o):
    c = CHUNK
    r = r_ref[0]
    v = v_ref[0]
    kk = kk_ref[0]
    n_pairs = r.shape[1] // LANES
    lane = lax.broadcasted_iota(jnp.int32, (c, LANES), 1)
    head0 = lane < HEAD_DIM
    ti = lax.broadcasted_iota(jnp.int32, (c, c), 0)
    si = lax.broadcasted_iota(jnp.int32, (c, c), 1)
    t2 = lax.broadcasted_iota(jnp.int32, (c, LANES), 0)
    s2 = lane % c
    eye = s2 == t2
    levels = int(math.log2(c))
    pair = lambda x, p: x[:, p * LANES:(p + 1) * LANES]
    bd16 = lambda x: _stack_bd(x, head0).astype(BF16)
    v_pairs = [bd16(pair(v, p)) for p in range(n_pairs)]
    wc, a_c, r_c, b_c, k_c, v_bd, past, incl, dest = [], [], [], [], [], [], [], [], []
    for d, (lw_ref, km_ref, a_ref) in enumerate(((lwf_ref, kmf_ref, af_ref), (lwb_ref, kmb_ref, ab_ref))):
        lw = lw_ref[0]
        if d == 0:
            tri = (si <= ti).astype(F32)
            past_d = s2 < t2
            last = c - 1
        else:
            tri = (si >= ti).astype(F32)
            past_d = s2 > t2
            last = 0
        incl_d = past_d | eye
        cs = _dot(tri, lw, HIGHEST)
        e_pos = jnp.exp(cs)
        e_neg = jnp.exp(-cs)
        a_all = -kk * jnp.exp(cs - lw)
        b_all = kk * a_ref[0] * e_neg
        k_all = km_ref[0] * e_neg
        r_all = r * e_pos
        wc_all = e_pos[last:last + 1, :]
        for p in range(n_pairs):
            wc.append(pair(wc_all, p))
            a_c.append(pair(a_all, p))
            r_c.append(pair(r_all, p))
            b_c.append(pair(b_all, p))
            k_c.append(pair(k_all, p))
            v_bd.append(v_pairs[p])
            past.append(past_d)
            incl.append(incl_d)
            dest.append((d, p))
    cs_ = range(len(dest))
    w = LANES
    b_bd = [_stack_bd(b_c[i], head0) for i in cs_]
    k_bd = [_stack_bd(k_c[i], head0) for i in cs_]
    gm = [_dot_nt(jnp.concatenate([a_c[i], r_c[i]], axis=0).astype(BF16),
                  jnp.concatenate([b_bd[i], k_bd[i]], axis=0).astype(BF16)) for i in cs_]
    aab = [jnp.where(past[i], gm[i][0:c, 0:w], 0.0) for i in cs_]
    aak = [jnp.where(past[i], gm[i][0:c, w:2 * w], 0.0) for i in cs_]
    mrb = [jnp.where(incl[i], gm[i][c:2 * c, 0:w], 0.0) for i in cs_]
    mrk = [jnp.where(incl[i], gm[i][c:2 * c, w:2 * w], 0.0) for i in cs_]
    bw_t = [_unstack_bd(jnp.transpose(b_bd[i] * wc[i])) for i in cs_]
    kw_t = [_unstack_bd(jnp.transpose(k_bd[i] * wc[i])) for i in cs_]
    xv = [_dot(jnp.concatenate([aak[i], mrk[i], kw_t[i]], axis=0).astype(BF16), v_bd[i]) for i in cs_]
    tinv = [jnp.where(eye, 1.0, aab[i]) for i in cs_]
    xp = [_dot(aab[i].astype(BF16), bd16(aab[i])) for i in cs_]
    for _ in range(2, levels):
        res = [_dot(jnp.concatenate([xp[i], tinv[i]], axis=0).astype(BF16), bd16(xp[i])) for i in cs_]
        xp = [res[i][0:c] for i in cs_]
        tinv = [tinv[i] + res[i][c:2 * c] for i in cs_]
    tinv = [tinv[i] + _dot(tinv[i].astype(BF16), bd16(xp[i])) for i in cs_]
    z = [_dot(tinv[i].astype(BF16), jnp.concatenate([bd16(a_c[i]), bd16(xv[i][0:c])], axis=1)) for i in cs_]
    z_bd = [jnp.concatenate([bd16(z[i][:, 0:w]), bd16(z[i][:, w:2 * w])], axis=1) for i in cs_]
    qg = [_dot(jnp.concatenate([mrb[i], bw_t[i]], axis=0).astype(BF16), z_bd[i]) for i in cs_]
    for i, (d, p) in enumerate(dest):
        q1_o[d, 0, p] = r_c[i] + qg[i][0:c, 0:w]
        q2_o[d, 0, p] = qg[i][0:c, w:2 * w] + xv[i][c:2 * c]
        g1_o[d, 0, p] = jnp.where(eye, wc[i], 0.0) + qg[i][c:2 * c, 0:w]
        g2_o[d, 0, p] = qg[i][c:2 * c, w:2 * w] + xv[i][2 * c:3 * c]


def _prep_call(r, v, kk, lwf, lwb, kmf, kmb, af, ab):
    b, t, w = r.shape
    n_pairs = w // LANES
    nch = t // CHUNK
    in_spec = pl.BlockSpec((1, CHUNK, w), lambda bi, j: (bi, j, 0))
    out_spec = pl.BlockSpec((2, 1, n_pairs, CHUNK, LANES), lambda bi, j: (0, j, bi, 0, 0))
    out_sd = jax.ShapeDtypeStruct((2, nch, b * n_pairs, CHUNK, LANES), F32)
    return pl.pallas_call(
        _prep_kernel,
        grid=(b, nch),
        in_specs=[in_spec] * 9,
        out_specs=[out_spec] * 4,
        out_shape=[out_sd] * 4,
        compiler_params=_cparams(("parallel", "parallel")),
        name="prep",
    )(r, v, kk, lwf, lwb, kmf, kmb, af, ab)


def _scan_kernel(g1f, g2f, q1f, q2f, g1b, g2b, q1b, q2b, yf_o, yb_o, hf_s, hb_s):
    @pl.when(pl.program_id(0) == 0)
    def _():
        hf_s[...] = jnp.zeros_like(hf_s)
        hb_s[...] = jnp.zeros_like(hb_s)

    lane = lax.broadcasted_iota(jnp.int32, (CHUNK, LANES), 1)
    head0 = lane < HEAD_DIM
    n_pairs = hf_s.shape[0]
    c = CHUNK
    for g1, g2, q1, q2, y_o, h_s in ((g1f, g2f, q1f, q2f, yf_o, hf_s), (g1b, g2b, q1b, q2b, yb_o, hb_s)):
        for p in range(n_pairs):
            lhs = jnp.concatenate([g1[0, 0, p], q1[0, 0, p]], axis=0)
            res = _dot_split3(lhs, _stack_bd(h_s[p], head0))
            h_s[p] = res[0:c] + g2[0, 0, p]
            y_o[0, p] = res[c:2 * c] + q2[0, 0, p]


def _scan_call(g1, g2, q1, q2, n_ctx_chunks):
    _, nch, npair, c, lanes = g1.shape
    n_lat_chunks = nch - n_ctx_chunks
    fwd = lambda s: (0, s, 0, 0, 0)
    bwd_chunk = lambda s: jnp.where(s < n_ctx_chunks, n_ctx_chunks - 1 - s, 2 * n_ctx_chunks + n_lat_chunks - 1 - s)
    bwd = lambda s: (1, bwd_chunk(s), 0, 0, 0)
    blk = (1, 1, npair, c, lanes)
    y_sd = jax.ShapeDtypeStruct((nch, npair, c, lanes), F32)
    return pl.pallas_call(
        _scan_kernel,
        grid=(nch,),
        in_specs=[pl.BlockSpec(blk, fwd)] * 4 + [pl.BlockSpec(blk, bwd)] * 4,
        out_specs=[pl.BlockSpec((1, npair, c, lanes), lambda s: (s, 0, 0, 0)),
                   pl.BlockSpec((1, npair, c, lanes), lambda s: (bwd_chunk(s), 0, 0, 0))],
        out_shape=[y_sd, y_sd],
        scratch_shapes=[pltpu.VMEM((npair, c, lanes), F32), pltpu.VMEM((npair, c, lanes), F32)],
        compiler_params=_cparams(("arbitrary",)),
        name="scan",
    )(g1, g2, q1, q2, g1, g2, q1, q2)


def _merge_kernel(yf_ref, yb_ref, bonus_ref, g_ref, zg_ref, yna_ref, x_ref, gt1_ref, sh2_ref, sc2_ref,
                  wpa_ref, wpr_ref, wo_ref, gnw_ref, gnb_ref, l1g_ref, l1b_ref, wrt_ref, ones_ref,
                  h1_o, u2_o, aff_o):
    tm = x_ref.shape[1]
    n_pairs = yf_ref.shape[1]
    ys = yf_ref[...] + yb_ref[...]
    y = jnp.concatenate([ys[:, p].reshape(tm, LANES) for p in range(n_pairs)], axis=1)
    ones_bd = ones_ref[...]
    inv = 1.0 / HEAD_DIM
    mu = _head_sum(y, ones_bd) * inv
    yc = y - mu
    var = _head_sum(yc * yc, ones_bd) * inv
    yn = yc * lax.rsqrt(var + GN_EPS) * gnw_ref[...] + gnb_ref[...]
    y_rw = (yn + bonus_ref[0]) * g_ref[0]
    gates = jax.nn.sigmoid(zg_ref[0].astype(F32))
    d = x_ref.shape[2]
    m1 = gates[:, 0:d] * _dot(yna_ref[0], wpa_ref[...]) + gates[:, d:2 * d] * _dot(y_rw.astype(BF16), wpr_ref[...])
    m = _dot(m1.astype(BF16), wo_ref[...])
    h1 = _ln(ALPHA * x_ref[0] + gt1_ref[0] * m, LN_EPS) * l1g_ref[...] + l1b_ref[...]
    h1_o[0] = h1
    u2 = _ln(h1, LN_EPS) * (1.0 + sc2_ref[0]) + sh2_ref[0]
    u2_o[0] = u2.astype(BF16)
    logits = _dot_nt(wrt_ref[...], u2, HIGHEST)
    mx = jnp.max(logits, axis=0, keepdims=True)
    e = jnp.exp(logits - mx)
    aff_o[0] = e / jnp.sum(e, axis=0, keepdims=True)


def _merge_call(yf, yb, bonus, g, zgate, y_na, x, gt1, sh2, sc2, w_pa, w_pr, w_o, gn_w, gn_b, l1g, l1b, w_rt,
                ones_bd, n_ctx):
    b, n, d = x.shape
    tm = ROW_TILE
    cpt = tm // CHUNK
    n_pairs = RW_WIDTH // LANES
    ctx_tiles = n_ctx // tm
    e = w_rt.shape[0]
    y_spec = pl.BlockSpec((cpt, n_pairs, CHUNK, LANES), lambda bi, i: (ctx_tiles + i, bi, 0, 0))
    full = lambda shape: pl.BlockSpec(shape, lambda bi, i: (0,) * len(shape))
    modv = pl.BlockSpec((1, 1, d), lambda bi, i: (bi, 0, 0))
    return pl.pallas_call(
        _merge_kernel,
        grid=(b, n // tm),
        in_specs=[y_spec, y_spec,
                  pl.BlockSpec((1, tm, RW_WIDTH), lambda bi, i: (bi, ctx_tiles + i, 0)),
                  pl.BlockSpec((1, tm, RW_WIDTH), lambda bi, i: (bi, ctx_tiles + i, 0)),
                  pl.BlockSpec((1, tm, 2 * d), lambda bi, i: (bi, ctx_tiles + i, 0)),
                  pl.BlockSpec((1, tm, NA_WIDTH), lambda bi, i: (bi, i, 0)),
                  pl.BlockSpec((1, tm, d), lambda bi, i: (bi, i, 0)),
                  modv, modv, modv,
                  full((NA_WIDTH, d)), full((RW_WIDTH, d)), full((d, d)),
                  full((1, RW_WIDTH)), full((1, RW_WIDTH)), full((1, d)), full((1, d)), full((e, d)),
                  full((RW_WIDTH, RW_WIDTH))],
        out_specs=[pl.BlockSpec((1, tm, d), lambda bi, i: (bi, i, 0)),
                   pl.BlockSpec((1, tm, d), lambda bi, i: (bi, i, 0)),
                   pl.BlockSpec((1, e, tm), lambda bi, i: (bi, 0, i))],
        out_shape=[jax.ShapeDtypeStruct((b, n, d), F32), jax.ShapeDtypeStruct((b, n, d), BF16),
                   jax.ShapeDtypeStruct((b, e, n), F32)],
        compiler_params=_cparams(("parallel", "parallel")),
        name="merge",
    )(yf, yb, bonus, g, zgate, y_na, x, gt1, sh2, sc2, w_pa, w_pr, w_o, gn_w, gn_b, l1g, l1b, w_rt, ones_bd)


def _ffn_kernel(x_ref, gate_ref, w1_ref, w3_ref, w2_ref, o_ref, hdn_ref, *, n_up):
    s = pl.program_id(2)

    @pl.when(s < n_up)
    def _():
        xs = x_ref[0, 0]
        a = _dot(xs, w1_ref[0].astype(BF16))
        bb = _dot(xs, w3_ref[0].astype(BF16))
        hdn_ref[s] = (a * jax.nn.sigmoid(a) * bb).astype(BF16)

    @pl.when(s >= n_up)
    def _():
        tf = hdn_ref.shape[2]
        acc = _dot(hdn_ref[0], w2_ref[0, 0:tf, :].astype(BF16))
        for f in range(1, n_up):
            acc = acc + _dot(hdn_ref[f], w2_ref[0, f * tf:(f + 1) * tf, :].astype(BF16))
        o_ref[0, 0] = acc * gate_ref[0, 0]


def _ffn_call(xs, gate, w1, w3, w2):
    b, e, c, d = xs.shape
    fdim = w1.shape[2]
    tf = FFN_TILE
    tn = FFN_OUT_TILE
    n_up = fdim // tf
    up_idx = lambda ei, bi, s: (ei, 0, jnp.minimum(s, n_up - 1))
    down_idx = lambda s: jnp.maximum(s - n_up, 0)
    return pl.pallas_call(
        functools.partial(_ffn_kernel, n_up=n_up),
        grid=(e, b, n_up + d // tn),
        in_specs=[pl.BlockSpec((1, 1, c, d), lambda ei, bi, s: (bi, ei, 0, 0)),
                  pl.BlockSpec((1, 1, c, 1), lambda ei, bi, s: (bi, ei, 0, 0)),
                  pl.BlockSpec((1, d, tf), up_idx),
                  pl.BlockSpec((1, d, tf), up_idx),
                  pl.BlockSpec((1, fdim, tn), lambda ei, bi, s: (ei, 0, down_idx(s)))],
        out_specs=pl.BlockSpec((1, 1, c, tn), lambda ei, bi, s: (bi, ei, 0, down_idx(s))),
        out_shape=jax.ShapeDtypeStruct((b, e, c, d), F32),
        scratch_shapes=[pltpu.VMEM((n_up, c, tf), BF16)],
        compiler_params=_cparams(("parallel", "parallel", "arbitrary")),
        name="ffn",
    )(xs, gate, w1, w3, w2)


def _final_kernel(h_ref, moe_ref, gt2_ref, g_ref, b_ref, o_ref):
    o_ref[0] = _ln(ALPHA * h_ref[0] + gt2_ref[0] * moe_ref[0], LN_EPS) * g_ref[...] + b_ref[...]


def _final_call(h1, moe, gt2, l2g, l2b):
    b, n, d = h1.shape
    tm = ROW_TILE
    row = pl.BlockSpec((1, tm, d), lambda bi, i: (bi, i, 0))
    vec = pl.BlockSpec((1, d), lambda bi, i: (0, 0))
    return pl.pallas_call(
        _final_kernel,
        grid=(b, n // tm),
        in_specs=[row, row, pl.BlockSpec((1, 1, d), lambda bi, i: (bi, 0, 0)), vec, vec],
        out_specs=row,
        out_shape=jax.ShapeDtypeStruct((b, n, d), F32),
        compiler_params=_cparams(("parallel", "parallel")),
        name="final",
    )(h1, moe, gt2, l2g, l2b)


def _rope_tables(n_ctx, n_lat):
    nf = HEAD_DIM // 4
    t = np.arange(n_lat)
    pos = np.stack([t // GRID_W, t % GRID_W], axis=-1).astype(np.float32)
    inv_freq = jnp.power(jnp.float32(ROPE_BASE), -jnp.arange(nf, dtype=F32) / nf)
    ang = jnp.asarray(pos)[:, :, None] * inv_freq
    cos = jnp.cos(ang)
    sin = jnp.sin(ang)
    cos_h = jnp.concatenate([cos[:, 0], cos[:, 0], cos[:, 1], cos[:, 1]], axis=-1)
    sin_h = jnp.concatenate([-sin[:, 0], sin[:, 0], -sin[:, 1], sin[:, 1]], axis=-1)
    cos_t = jnp.concatenate([jnp.ones((n_ctx, HEAD_DIM), F32), cos_h], axis=0)
    sin_t = jnp.concatenate([jnp.zeros((n_ctx, HEAD_DIM), F32), sin_h], axis=0)
    return jnp.tile(cos_t, (1, LANES // HEAD_DIM)), jnp.tile(sin_t, (1, LANES // HEAD_DIM))


def kernel(x, c, ctx, c_ctx, w_mod, b_mod, w_in, rpb, mu_prev, mu_next, w0, w_up, a0, a_up, g_up, k_k, k_a, r_k,
           gn_w, gn_b, w_pa, w_pr, w_o, ln1_g, ln1_b, w_router, w_e1, w_e3, w_e2, ln2_g, ln2_b):
    assert w_mod.shape[0] == DEPTH
    b, n, d = x.shape
    n_ctx = ctx.shape[1]
    assert n_ctx % ROW_TILE == 0 and n % ROW_TILE == 0 and (n // GRID_W) % NA_ROWS == 0
    t = n_ctx + n
    w = RW_WIDTH

    mod_rows = 8 * ((b + 1 + 7) // 8)
    cvec = jnp.zeros((mod_rows, d), F32).at[:b].set(c).at[b].set(c_ctx)
    mod = _mod_call(cvec, w_mod[0], b_mod[0][None])[:b + 1]
    sh1, sc1, gt1, sh2, sc2, gt2 = [mod[:, None, i * d:(i + 1) * d] for i in range(6)]

    cos_tab, sin_tab = _rope_tables(n_ctx, n)
    q_all, k_all, v_all, zrw, zgate, first_rows, last_rows = _proj_call(
        ctx, x, sh1, sc1, w_in[0].astype(BF16), cos_tab, sin_tab)

    y_na = _natten_call(q_all, k_all, v_all, _natten_bias_table(rpb[0]), n_ctx, n)

    nt = t // ROW_TILE
    ctx_tiles = n_ctx // ROW_TILE
    zero_row = jnp.zeros((b, 1, 1, RW_COLS), F32)
    prev_rows = jnp.concatenate([zero_row, last_rows[:, :nt - 1]], axis=1)
    next_rows = jnp.concatenate([first_rows[:, 1:], zero_row], axis=1)
    seq_start = (np.arange(nt) == 0) | (np.arange(nt) == ctx_tiles)
    seq_end = (np.arange(nt) == ctx_tiles - 1) | (np.arange(nt) == nt - 1)
    prev_rows = jnp.where(jnp.asarray(seq_start)[None, :, None, None], 0.0, prev_rows)
    next_rows = jnp.where(jnp.asarray(seq_end)[None, :, None, None], 0.0, next_rows)
    zpad = jnp.zeros((DECAY_LORA, w), F32)
    wup_pad = jnp.stack([jnp.concatenate([w_up[0, 0], zpad], 0), jnp.concatenate([zpad, w_up[0, 1]], 0)])
    aup_pad = jnp.stack([jnp.concatenate([a_up[0, 0], zpad], 0), jnp.concatenate([zpad, a_up[0, 1]], 0)])
    ones_bd = jnp.asarray(np.kron(np.eye(w // HEAD_DIM), np.ones((HEAD_DIM, HEAD_DIM))), BF16)
    r, v, kk, lwf, lwb, kmf, kmb, af, ab, g, bonus = _feat_call(
        zrw, prev_rows, next_rows, mu_prev, mu_next, k_k, k_a, r_k[0].reshape(1, w), w0[0][:, None, :], wup_pad,
        a0[0][:, None, :], aup_pad.astype(BF16), g_up[0].astype(BF16), ones_bd)

    g1, g2, q1, q2 = _prep_call(r, v, kk, lwf, lwb, kmf, kmb, af, ab)
    yf, yb = _scan_call(g1, g2, q1, q2, n_ctx // CHUNK)

    h1, u2, aff_t = _merge_call(yf, yb, bonus, g, zgate, y_na, x, gt1[:b], sh2[:b], sc2[:b],
                                w_pa[0].astype(BF16), w_pr[0].astype(BF16), w_o[0].astype(BF16),
                                gn_w, gn_b, ln1_g, ln1_b, jnp.transpose(w_router[0]), ones_bd, n_ctx)

    cap = CAPACITY_FACTOR * n // N_EXPERTS
    gate, idx = lax.top_k(aff_t, cap)
    flat_idx = (idx + (jnp.arange(b, dtype=jnp.int32) * n)[:, None, None]).reshape(-1)
    xs = jnp.take(u2.reshape(b * n, d), flat_idx, axis=0).reshape(b, N_EXPERTS, cap, d)
    ye = _ffn_call(xs, gate[..., None], w_e1[0], w_e3[0], w_e2[0])
    moe = jnp.zeros((b * n, d), F32).at[flat_idx].add(ye.reshape(-1, d)).reshape(b, n, d)

    return _final_call(h1, moe, gt2[:b], ln2_g, ln2_b)
```

```python
import functools
import math

import jax
import jax.numpy as jnp
import numpy as np
from jax import lax
from jax.experimental import pallas as pl
from jax.experimental.pallas import tpu as pltpu

F32 = jnp.float32
BF16 = jnp.bfloat16
HIGHEST = lax.Precision.HIGHEST

GRID_W = 64
NA_HEADS = 8
HEAD_DIM = 64
NA_WIDTH = NA_HEADS * HEAD_DIM
WIN_H = 8
WIN_W = 16
ROPE_BASE = 10000.0
RW_WIDTH = 512
DECAY_LORA = 64
AAA_LORA = 64
GATE_LORA = 128
RW_COLS = 3 * RW_WIDTH + 2 * DECAY_LORA + 2 * AAA_LORA + GATE_LORA
N_EXPERTS = 16
CAPACITY_FACTOR = 2
DEPTH = 1
ALPHA = (2.0 * DEPTH) ** 0.25
LN_EPS = 1e-6
GN_EPS = 64e-5
NEG_BIG = -1e30

LANES = 128
CHUNK = 64
ROW_TILE = 256
NA_ROWS = 4
FFN_TILE = 256
VMEM_LIMIT = 48 * 1024 * 1024


def _cparams(sem):
    return pltpu.CompilerParams(dimension_semantics=sem, vmem_limit_bytes=VMEM_LIMIT)


def _dot(a, b, prec=None):
    return jnp.dot(a, b, preferred_element_type=F32, precision=prec)


def _dot_nt(a, b, prec=None):
    return lax.dot_general(a, b, (((1,), (1,)), ((), ())), preferred_element_type=F32, precision=prec)


def _dot_split3(a, b):
    a_hi = a.astype(BF16)
    a_lo = (a - a_hi.astype(F32)).astype(BF16)
    b_hi = b.astype(BF16)
    b_lo = (b - b_hi.astype(F32)).astype(BF16)
    m = a.shape[0]
    hh_lh = _dot(jnp.concatenate([a_hi, a_lo], axis=0), b_hi)
    return hh_lh[0:m] + hh_lh[m:2 * m] + _dot(a_hi, b_lo)


def _ln(x, eps):
    mu = jnp.mean(x, axis=-1, keepdims=True)
    xc = x - mu
    var = jnp.mean(xc * xc, axis=-1, keepdims=True)
    return xc * lax.rsqrt(var + eps)


def _head_sum(x, ones_bd):
    hi = x.astype(BF16)
    lo = (x - hi.astype(F32)).astype(BF16)
    return _dot(hi, ones_bd) + _dot(lo, ones_bd)


def _mod_kernel(c_ref, w_ref, b_ref, o_ref):
    cv = c_ref[...]
    s = cv * jax.nn.sigmoid(cv)
    o_ref[...] = _dot(s, w_ref[...], HIGHEST) + b_ref[...]


def _mod_call(cvec, w_mod, b_mod):
    rows, d = cvec.shape
    n = w_mod.shape[1]
    tn = 1536
    return pl.pallas_call(
        _mod_kernel,
        grid=(n // tn,),
        in_specs=[pl.BlockSpec((rows, d), lambda j: (0, 0)),
                  pl.BlockSpec((d, tn), lambda j: (0, j)),
                  pl.BlockSpec((1, tn), lambda j: (0, j))],
        out_specs=pl.BlockSpec((rows, tn), lambda j: (0, j)),
        out_shape=jax.ShapeDtypeStruct((rows, n), F32),
        compiler_params=_cparams(("parallel",)),
        name="mod",
    )(cvec, w_mod, b_mod)


def _rope(z, cos, sin, x1_lane):
    outs = []
    for j in range(z.shape[1] // LANES):
        zj = z[:, j * LANES:(j + 1) * LANES]
        partner = jnp.where(x1_lane, pltpu.roll(zj, LANES - 16, 1), pltpu.roll(zj, 16, 1))
        outs.append(zj * cos + partner * sin)
    return jnp.concatenate(outs, axis=1)


def _proj_kernel(ctx_ref, x_ref, sh_ref, sc_ref, w_ref, cos_ref, sin_ref,
                 q_ref, k_ref, v_ref, rw_ref, g_ref, first_ref, last_ref, *, n_ctx_tiles):
    x = jnp.where(pl.program_id(1) < n_ctx_tiles, ctx_ref[0], x_ref[0])
    u = (_ln(x, LN_EPS) * (1.0 + sc_ref[0]) + sh_ref[0]).astype(BF16)
    cos = cos_ref[...]
    sin = sin_ref[...]
    lane = lax.broadcasted_iota(jnp.int32, cos.shape, 1)
    x1_lane = (lane % 32) < 16
    nw = NA_WIDTH
    zq = _dot(u, w_ref[:, 0:nw])
    q_ref[0] = (_rope(zq, cos, sin, x1_lane) * (HEAD_DIM ** -0.5)).astype(BF16)
    zk = _dot(u, w_ref[:, nw:2 * nw])
    k_ref[0] = _rope(zk, cos, sin, x1_lane).astype(BF16)
    v_ref[0] = _dot(u, w_ref[:, 2 * nw:3 * nw]).astype(BF16)
    zrw = _dot(u, w_ref[:, 3 * nw:3 * nw + RW_COLS])
    rw_ref[0] = zrw
    first_ref[0, 0] = zrw[0:1]
    last_ref[0, 0] = zrw[zrw.shape[0] - 1:]
    g_ref[0] = _dot(u, w_ref[:, 3 * nw + RW_COLS:]).astype(BF16)


def _proj_call(ctx, x, sh_tab, sc_tab, w_in_bf, cos_tab, sin_tab):
    b, n_ctx, d = ctx.shape
    t = n_ctx + x.shape[1]
    p_in = w_in_bf.shape[1]
    gate_cols = p_in - 3 * NA_WIDTH - RW_COLS
    tm = ROW_TILE
    n_ctx_tiles = n_ctx // tm
    nt = t // tm
    mod_idx = lambda bi, i: (jnp.where(i < n_ctx_tiles, b, bi), 0, 0)
    row_blk = lambda w: pl.BlockSpec((1, tm, w), lambda bi, i: (bi, i, 0))
    edge_blk = pl.BlockSpec((1, 1, 1, RW_COLS), lambda bi, i: (bi, i, 0, 0))
    edge_sd = jax.ShapeDtypeStruct((b, nt, 1, RW_COLS), F32)
    return pl.pallas_call(
        functools.partial(_proj_kernel, n_ctx_tiles=n_ctx_tiles),
        grid=(b, nt),
        in_specs=[pl.BlockSpec((1, tm, d), lambda bi, i: (bi, jnp.minimum(i, n_ctx_tiles - 1), 0)),
                  pl.BlockSpec((1, tm, d), lambda bi, i: (bi, jnp.maximum(i - n_ctx_tiles, 0), 0)),
                  pl.BlockSpec((1, 1, d), mod_idx),
                  pl.BlockSpec((1, 1, d), mod_idx),
                  pl.BlockSpec((d, p_in), lambda bi, i: (0, 0)),
                  pl.BlockSpec((tm, LANES), lambda bi, i: (i, 0)),
                  pl.BlockSpec((tm, LANES), lambda bi, i: (i, 0))],
        out_specs=[row_blk(NA_WIDTH), row_blk(NA_WIDTH), row_blk(NA_WIDTH), row_blk(RW_COLS), row_blk(gate_cols),
                   edge_blk, edge_blk],
        out_shape=[jax.ShapeDtypeStruct((b, t, NA_WIDTH), BF16)] * 3
        + [jax.ShapeDtypeStruct((b, t, RW_COLS), F32), jax.ShapeDtypeStruct((b, t, gate_cols), BF16),
           edge_sd, edge_sd],
        compiler_params=_cparams(("parallel", "parallel")),
        name="proj",
    )(ctx, x, sh_tab, sc_tab, w_in_bf, cos_tab, sin_tab)


def _natten_kernel(q_ref, k_ref, v_ref, bias_ref, o_ref, *, n_ctx, n_rows):
    r = pl.program_id(2)
    kc = k_ref[0, 0:n_ctx, :]
    vc = v_ref[0, 0:n_ctx, :]
    lane2 = lax.broadcasted_iota(jnp.int32, (GRID_W, LANES), 1)
    head0 = lane2 < HEAD_DIM
    for ii in range(NA_ROWS):
        i = r * NA_ROWS + ii
        rs = jnp.clip(i - WIN_H // 2, 0, n_rows - WIN_H)
        off = rs - i + (WIN_H - 1)
        q = q_ref[0, ii * GRID_W:(ii + 1) * GRID_W, :]
        zero = jnp.zeros_like(q)
        q2 = jnp.concatenate([jnp.where(head0, q, zero), jnp.where(head0, zero, q)], axis=0)
        start = pl.multiple_of(n_ctx + rs * GRID_W, GRID_W)
        kw = k_ref[0, pl.ds(start, WIN_H * GRID_W), :]
        vw = v_ref[0, pl.ds(start, WIN_H * GRID_W), :]
        s_loc = _dot_nt(q2, kw) + bias_ref[0, off]
        s_ctx = _dot_nt(q2, kc)
        m = jnp.maximum(jnp.max(s_loc, axis=-1, keepdims=True), jnp.max(s_ctx, axis=-1, keepdims=True))
        p_loc = jnp.exp(s_loc - m)
        p_ctx = jnp.exp(s_ctx - m)
        denom = jnp.sum(p_loc, axis=-1, keepdims=True) + jnp.sum(p_ctx, axis=-1, keepdims=True)
        o2 = (_dot(p_loc.astype(BF16), vw) + _dot(p_ctx.astype(BF16), vc)) / denom
        out = jnp.where(head0, o2[0:GRID_W], o2[GRID_W:2 * GRID_W])
        o_ref[0, ii * GRID_W:(ii + 1) * GRID_W, :] = out.astype(o_ref.dtype)


def _natten_call(q_all, k_all, v_all, bias2, n_ctx, n_lat):
    b, t, _ = q_all.shape
    n_rows = n_lat // GRID_W
    blk = NA_ROWS * GRID_W
    ctx_blocks = n_ctx // blk
    n_pairs = NA_HEADS // 2
    kv_spec = pl.BlockSpec((1, t, LANES), lambda bi, hp, r: (bi, 0, hp))
    return pl.pallas_call(
        functools.partial(_natten_kernel, n_ctx=n_ctx, n_rows=n_rows),
        grid=(b, n_pairs, n_rows // NA_ROWS),
        in_specs=[pl.BlockSpec((1, blk, LANES), lambda bi, hp, r: (bi, ctx_blocks + r, hp)),
                  kv_spec, kv_spec,
                  pl.BlockSpec((1, WIN_H, 2 * GRID_W, WIN_H * GRID_W), lambda bi, hp, r: (hp, 0, 0, 0))],
        out_specs=pl.BlockSpec((1, blk, LANES), lambda bi, hp, r: (bi, r, hp)),
        out_shape=jax.ShapeDtypeStruct((b, n_lat, NA_WIDTH), BF16),
        compiler_params=_cparams(("parallel", "parallel", "arbitrary")),
        name="natten",
    )(q_all, k_all, v_all, bias2)


def _natten_bias_table(rpb):
    col = np.arange(GRID_W)
    col_start = np.clip(col - WIN_W // 2, 0, GRID_W - WIN_W)
    in_win = (col[None, :] >= col_start[:, None]) & (col[None, :] < col_start[:, None] + WIN_W)
    col_off = np.clip(col[None, :] - col[:, None] + (WIN_W - 1), 0, 2 * WIN_W - 2)
    row_off = np.arange(WIN_H)[:, None] + np.arange(WIN_H)[None, :]
    tab = rpb[:, row_off][:, :, :, col_off]
    tab = jnp.where(jnp.asarray(in_win)[None, None, None], tab, NEG_BIG)
    tab = jnp.transpose(tab, (0, 1, 3, 2, 4))
    h = rpb.shape[0]
    tab = tab.reshape(h // 2, 2, WIN_H, GRID_W, WIN_H * GRID_W)
    tab = jnp.transpose(tab, (0, 2, 1, 3, 4)).reshape(h // 2, WIN_H, 2 * GRID_W, WIN_H * GRID_W)
    return tab.astype(F32)


def _feat_kernel(z_ref, prev_ref, next_ref, mup_ref, mun_ref, kk_ref, ka_ref, rk_ref, w0_ref, wup_ref, a0_ref,
                 aup_ref, gup_ref, ones_ref,
                 r_o, v_o, kk_o, lwf_o, lwb_o, kmf_o, kmb_o, af_o, ab_o, g_o, bonus_o):
    z = z_ref[0]
    tm = z.shape[0]
    row = lax.broadcasted_iota(jnp.int32, (tm, 1), 0)
    zp = jnp.where(row == 0, prev_ref[0, 0], pltpu.roll(z, 1, 0))
    zn = jnp.where(row == tm - 1, next_ref[0, 0], pltpu.roll(z, tm - 1, 0))
    zs = z + mup_ref[...] * (zp - z) + mun_ref[...] * (zn - z)
    w = RW_WIDTH
    r = zs[:, 0:w]
    k = zs[:, w:2 * w]
    v = zs[:, 2 * w:3 * w]
    wd = jnp.tanh(zs[:, 3 * w:3 * w + 2 * DECAY_LORA])
    ad = zs[:, 3 * w + 2 * DECAY_LORA:3 * w + 2 * DECAY_LORA + 2 * AAA_LORA].astype(BF16)
    gd = jax.nn.sigmoid(zs[:, 3 * w + 2 * DECAY_LORA + 2 * AAA_LORA:]).astype(BF16)
    ones_bd = ones_ref[...]
    kkf = k * kk_ref[...]
    kk = kkf * lax.rsqrt(_head_sum(kkf * kkf, ones_bd) + 1e-12)
    r_o[0] = r.astype(BF16)
    v_o[0] = v.astype(BF16)
    kk_o[0] = kk.astype(BF16)
    g_o[0] = _dot(gd, gup_ref[...]).astype(BF16)
    rk = r * rk_ref[...]
    acc = jnp.zeros_like(r)
    for d, (lw_o, km_o, a_o) in enumerate(((lwf_o, kmf_o, af_o), (lwb_o, kmb_o, ab_o))):
        w_raw = w0_ref[d] + _dot(wd, wup_ref[d], HIGHEST)
        lw_o[0] = -math.exp(-0.5) * jax.nn.sigmoid(w_raw)
        a = jax.nn.sigmoid(a0_ref[d] + _dot(ad, aup_ref[d]))
        kmod = k * (1.0 + (a - 1.0) * ka_ref[...])
        a_o[0] = a.astype(BF16)
        km_o[0] = kmod.astype(BF16)
        acc = acc + rk * kmod
    bonus_o[0] = (_head_sum(acc, ones_bd) * v).astype(BF16)


def _feat_call(zrw, prev_rows, next_rows, mu_prev, mu_next, k_k, k_a, r_k, w0, wup_pad, a0, aup_pad, g_up, ones_bd):
    b, t, c = zrw.shape
    tm = ROW_TILE
    w = RW_WIDTH
    full = lambda shape: pl.BlockSpec(shape, lambda bi, i: (0,) * len(shape))
    row = pl.BlockSpec((1, tm, w), lambda bi, i: (bi, i, 0))
    edge = pl.BlockSpec((1, 1, 1, c), lambda bi, i: (bi, i, 0, 0))
    return pl.pallas_call(
        _feat_kernel,
        grid=(b, t // tm),
        in_specs=[pl.BlockSpec((1, tm, c), lambda bi, i: (bi, i, 0)), edge, edge,
                  full((1, c)), full((1, c)), full((1, w)), full((1, w)), full((1, w)),
                  full((2, 1, w)), full((2, 2 * DECAY_LORA, w)), full((2, 1, w)), full((2, 2 * AAA_LORA, w)),
                  full((GATE_LORA, w)), full((w, w))],
        out_specs=[row] * 11,
        out_shape=[jax.ShapeDtypeStruct((b, t, w), dt) for dt in (BF16, BF16, BF16, F32, F32) + (BF16,) * 6],
        compiler_params=_cparams(("parallel", "parallel")),
        name="feat",
    )(zrw, prev_rows, next_rows, mu_prev, mu_next, k_k, k_a, r_k, w0, wup_pad, a0, aup_pad, g_up, ones_bd)


def _stack_bd(x, head0):
    zero = jnp.zeros_like(x)
    return jnp.concatenate([jnp.where(head0, x, zero), jnp.where(head0, zero, x)], axis=0)


def _unstack_bd(x):
    c = x.shape[0] // 2
    return x[0:c] + x[c:2 * c]


def _prep_kernel(r_ref, v_ref, kk_ref, lwf_ref, lwb_ref, kmf_ref, kmb_ref, af_ref, ab_ref,
                 g1_o, g2_o, q1_o, q2_o):
    c = CHUNK
    r = r_ref[0].astype(F32)
    v = v_ref[0].astype(F32)
    kk = kk_ref[0].astype(F32)
    n_pairs = r.shape[1] // LANES
    lane = lax.broadcasted_iota(jnp.int32, (c, LANES), 1)
    head0 = lane < HEAD_DIM
    ti = lax.broadcasted_iota(jnp.int32, (c, c), 0)
    si = lax.broadcasted_iota(jnp.int32, (c, c), 1)
    t2 = lax.broadcasted_iota(jnp.int32, (c, LANES), 0)
    s2 = lane % c
    eye = s2 == t2
    levels = int(math.log2(c))
    pair = lambda x, p: x[:, p * LANES:(p + 1) * LANES]
    bd16 = lambda x: _stack_bd(x, head0).astype(BF16)
    v_pairs = [bd16(pair(v, p)) for p in range(n_pairs)]
    wc, a_c, r_c, b_c, k_c, v_bd, past, incl, dest = [], [], [], [], [], [], [], [], []
    for d, (lw_ref, km_ref, a_ref) in enumerate(((lwf_ref, kmf_ref, af_ref), (lwb_ref, kmb_ref, ab_ref))):
        lw = lw_ref[0]
        if d == 0:
            tri = (si <= ti).astype(F32)
            past_d = s2 < t2
            last = c - 1
        else:
            tri = (si >= ti).astype(F32)
            past_d = s2 > t2
            last = 0
        incl_d = past_d | eye
        cs = _dot(tri, lw, HIGHEST)
        e_pos = jnp.exp(cs)
        e_neg = jnp.exp(-cs)
        a_all = -kk * jnp.exp(cs - lw)
        b_all = kk * a_ref[0].astype(F32) * e_neg
        k_all = km_ref[0].astype(F32) * e_neg
        r_all = r * e_pos
        wc_all = e_pos[last:last + 1, :]
        for p in range(n_pairs):
            wc.append(pair(wc_all, p))
            a_c.append(pair(a_all, p))
            r_c.append(pair(r_all, p))
            b_c.append(pair(b_all, p))
            k_c.append(pair(k_all, p))
            v_bd.append(v_pairs[p])
            past.append(past_d)
            incl.append(incl_d)
            dest.append((d, p))
    cs_ = range(len(dest))
    w = LANES
    b_bd = [_stack_bd(b_c[i], head0) for i in cs_]
    k_bd = [_stack_bd(k_c[i], head0) for i in cs_]
    gm = [_dot_nt(jnp.concatenate([a_c[i], r_c[i]], axis=0).astype(BF16),
                  jnp.concatenate([b_bd[i], k_bd[i]], axis=0).astype(BF16)) for i in cs_]
    aab = [jnp.where(past[i], gm[i][0:c, 0:w], 0.0) for i in cs_]
    aak = [jnp.where(past[i], gm[i][0:c, w:2 * w], 0.0) for i in cs_]
    mrb = [jnp.where(incl[i], gm[i][c:2 * c, 0:w], 0.0) for i in cs_]
    mrk = [jnp.where(incl[i], gm[i][c:2 * c, w:2 * w], 0.0) for i in cs_]
    bw_t = [_unstack_bd(jnp.transpose(b_bd[i] * wc[i])) for i in cs_]
    kw_t = [_unstack_bd(jnp.transpose(k_bd[i] * wc[i])) for i in cs_]
    xv = [_dot(jnp.concatenate([aak[i], mrk[i], kw_t[i]], axis=0).astype(BF16), v_bd[i]) for i in cs_]
    tinv = [jnp.where(eye, 1.0, aab[i]) for i in cs_]
    xp = [_dot(aab[i].astype(BF16), bd16(aab[i])) for i in cs_]
    for _ in range(2, levels):
        res = [_dot(jnp.concatenate([xp[i], tinv[i]], axis=0).astype(BF16), bd16(xp[i])) for i in cs_]
        xp = [res[i][0:c] for i in cs_]
        tinv = [tinv[i] + res[i][c:2 * c] for i in cs_]
    tinv = [tinv[i] + _dot(tinv[i].astype(BF16), bd16(xp[i])) for i in cs_]
    z = [_dot(tinv[i].astype(BF16), jnp.concatenate([bd16(a_c[i]), bd16(xv[i][0:c])], axis=1)) for i in cs_]
    z_bd = [jnp.concatenate([bd16(z[i][:, 0:w]), bd16(z[i][:, w:2 * w])], axis=1) for i in cs_]
    qg = [_dot(jnp.concatenate([mrb[i], bw_t[i]], axis=0).astype(BF16), z_bd[i]) for i in cs_]
    for i, (d, p) in enumerate(dest):
        q1_o[d, 0, p] = r_c[i] + qg[i][0:c, 0:w]
        q2_o[d, 0, p] = qg[i][0:c, w:2 * w] + xv[i][c:2 * c]
        g1_o[d, 0, p] = jnp.where(eye, wc[i], 0.0) + qg[i][c:2 * c, 0:w]
        g2_o[d, 0, p] = qg[i][c:2 * c, w:2 * w] + xv[i][2 * c:3 * c]


def _prep_call(r, v, kk, lwf, lwb, kmf, kmb, af, ab):
    b, t, w = r.shape
    n_pairs = w // LANES
    nch = t // CHUNK
    in_spec = pl.BlockSpec((1, CHUNK, w), lambda bi, j: (bi, j, 0))
    out_spec = pl.BlockSpec((2, 1, n_pairs, CHUNK, LANES), lambda bi, j: (0, j, bi, 0, 0))
    out_sd = jax.ShapeDtypeStruct((2, nch, b * n_pairs, CHUNK, LANES), F32)
    return pl.pallas_call(
        _prep_kernel,
        grid=(b, nch),
        in_specs=[in_spec] * 9,
        out_specs=[out_spec] * 4,
        out_shape=[out_sd] * 4,
        compiler_params=_cparams(("parallel", "parallel")),
        name="prep",
    )(r, v, kk, lwf, lwb, kmf, kmb, af, ab)


def _scan_kernel(g1f, g2f, q1f, q2f, g1b, g2b, q1b, q2b, yf_o, yb_o, hf_s, hb_s):
    @pl.when(pl.program_id(0) == 0)
    def _():
        hf_s[...] = jnp.zeros_like(hf_s)
        hb_s[...] = jnp.zeros_like(hb_s)

    lane = lax.broadcasted_iota(jnp.int32, (CHUNK, LANES), 1)
    head0 = lane < HEAD_DIM
    n_pairs = hf_s.shape[0]
    c = CHUNK
    for g1, g2, q1, q2, y_o, h_s in ((g1f, g2f, q1f, q2f, yf_o, hf_s), (g1b, g2b, q1b, q2b, yb_o, hb_s)):
        for p in range(n_pairs):
            lhs = jnp.concatenate([g1[0, 0, p], q1[0, 0, p]], axis=0)
            res = _dot_split3(lhs, _stack_bd(h_s[p], head0))
            h_s[p] = res[0:c] + g2[0, 0, p]
            y_o[0, p] = res[c:2 * c] + q2[0, 0, p]


def _scan_call(g1, g2, q1, q2, n_ctx_chunks):
    _, nch, npair, c, lanes = g1.shape
    n_lat_chunks = nch - n_ctx_chunks
    fwd = lambda s: (0, s, 0, 0, 0)
    bwd_chunk = lambda s: jnp.where(s < n_ctx_chunks, n_ctx_chunks - 1 - s, 2 * n_ctx_chunks + n_lat_chunks - 1 - s)
    bwd = lambda s: (1, bwd_chunk(s), 0, 0, 0)
    blk = (1, 1, npair, c, lanes)
    y_sd = jax.ShapeDtypeStruct((nch, npair, c, lanes), F32)
    return pl.pallas_call(
        _scan_kernel,
        grid=(nch,),
        in_specs=[pl.BlockSpec(blk, fwd)] * 4 + [pl.BlockSpec(blk, bwd)] * 4,
        out_specs=[pl.BlockSpec((1, npair, c, lanes), lambda s: (s, 0, 0, 0)),
                   pl.BlockSpec((1, npair, c, lanes), lambda s: (bwd_chunk(s), 0, 0, 0))],
        out_shape=[y_sd, y_sd],
        scratch_shapes=[pltpu.VMEM((npair, c, lanes), F32), pltpu.VMEM((npair, c, lanes), F32)],
        compiler_params=_cparams(("arbitrary",)),
        name="scan",
    )(g1, g2, q1, q2, g1, g2, q1, q2)


def _merge_kernel(yf_ref, yb_ref, bonus_ref, g_ref, zg_ref, yna_ref, x_ref, gt1_ref, sh2_ref, sc2_ref,
                  wpa_ref, wpr_ref, wo_ref, gnw_ref, gnb_ref, l1g_ref, l1b_ref, wrt_ref, ones_ref,
                  h1_o, u2_o, aff_o):
    tm = x_ref.shape[1]
    n_pairs = yf_ref.shape[1]
    ys = yf_ref[...] + yb_ref[...]
    y = jnp.concatenate([ys[:, p].reshape(tm, LANES) for p in range(n_pairs)], axis=1)
    ones_bd = ones_ref[...]
    inv = 1.0 / HEAD_DIM
    mu = _head_sum(y, ones_bd) * inv
    yc = y - mu
    var = _head_sum(yc * yc, ones_bd) * inv
    yn = yc * lax.rsqrt(var + GN_EPS) * gnw_ref[...] + gnb_ref[...]
    y_rw = (yn + bonus_ref[0].astype(F32)) * g_ref[0].astype(F32)
    gates = jax.nn.sigmoid(zg_ref[0].astype(F32))
    d = x_ref.shape[2]
    m1 = gates[:, 0:d] * _dot(yna_ref[0], wpa_ref[...]) + gates[:, d:2 * d] * _dot(y_rw.astype(BF16), wpr_ref[...])
    m = _dot(m1.astype(BF16), wo_ref[...])
    h1 = _ln(ALPHA * x_ref[0] + gt1_ref[0] * m, LN_EPS) * l1g_ref[...] + l1b_ref[...]
    h1_o[0] = h1
    u2 = _ln(h1, LN_EPS) * (1.0 + sc2_ref[0]) + sh2_ref[0]
    u2_o[0] = u2.astype(BF16)
    logits = _dot_nt(wrt_ref[...], u2, HIGHEST)
    mx = jnp.max(logits, axis=0, keepdims=True)
    e = jnp.exp(logits - mx)
    aff_o[0] = e / jnp.sum(e, axis=0, keepdims=True)


def _merge_call(yf, yb, bonus, g, zgate, y_na, x, gt1, sh2, sc2, w_pa, w_pr, w_o, gn_w, gn_b, l1g, l1b, w_rt,
                ones_bd, n_ctx):
    b, n, d = x.shape
    tm = ROW_TILE
    cpt = tm // CHUNK
    n_pairs = RW_WIDTH // LANES
    ctx_tiles = n_ctx // tm
    e = w_rt.shape[0]
    y_spec = pl.BlockSpec((cpt, n_pairs, CHUNK, LANES), lambda bi, i: (ctx_tiles + i, bi, 0, 0))
    full = lambda shape: pl.BlockSpec(shape, lambda bi, i: (0,) * len(shape))
    modv = pl.BlockSpec((1, 1, d), lambda bi, i: (bi, 0, 0))
    return pl.pallas_call(
        _merge_kernel,
        grid=(b, n // tm),
        in_specs=[y_spec, y_spec,
                  pl.BlockSpec((1, tm, RW_WIDTH), lambda bi, i: (bi, ctx_tiles + i, 0)),
                  pl.BlockSpec((1, tm, RW_WIDTH), lambda bi, i: (bi, ctx_tiles + i, 0)),
                  pl.BlockSpec((1, tm, 2 * d), lambda bi, i: (bi, ctx_tiles + i, 0)),
                  pl.BlockSpec((1, tm, NA_WIDTH), lambda bi, i: (bi, i, 0)),
                  pl.BlockSpec((1, tm, d), lambda bi, i: (bi, i, 0)),
                  modv, modv, modv,
                  full((NA_WIDTH, d)), full((RW_WIDTH, d)), full((d, d)),
                  full((1, RW_WIDTH)), full((1, RW_WIDTH)), full((1, d)), full((1, d)), full((e, d)),
                  full((RW_WIDTH, RW_WIDTH))],
        out_specs=[pl.BlockSpec((1, tm, d), lambda bi, i: (bi, i, 0)),
                   pl.BlockSpec((1, tm, d), lambda bi, i: (bi, i, 0)),
                   pl.BlockSpec((1, e, tm), lambda bi, i: (bi, 0, i))],
        out_shape=[jax.ShapeDtypeStruct((b, n, d), F32), jax.ShapeDtypeStruct((b, n, d), BF16),
                   jax.ShapeDtypeStruct((b, e, n), F32)],
        compiler_params=_cparams(("parallel", "parallel")),
        name="merge",
    )(yf, yb, bonus, g, zgate, y_na, x, gt1, sh2, sc2, w_pa, w_pr, w_o, gn_w, gn_b, l1g, l1b, w_rt, ones_bd)


def _ffn_kernel(x_ref, gate_ref, w1_ref, w3_ref, w2_ref, o_ref, acc_ref):
    f = pl.program_id(2)

    @pl.when(f == 0)
    def _():
        acc_ref[...] = jnp.zeros_like(acc_ref)

    xs = x_ref[0, 0]
    a = _dot(xs, w1_ref[0].astype(BF16))
    bb = _dot(xs, w3_ref[0].astype(BF16))
    hdn = (a * jax.nn.sigmoid(a) * bb).astype(BF16)
    acc_ref[...] += _dot(hdn, w2_ref[0].astype(BF16))

    @pl.when(f == pl.num_programs(2) - 1)
    def _():
        o_ref[0, 0] = acc_ref[...] * gate_ref[0, 0]


def _ffn_call(xs, gate, w1, w3, w2):
    b, e, c, d = xs.shape
    fdim = w1.shape[2]
    tf = FFN_TILE
    return pl.pallas_call(
        _ffn_kernel,
        grid=(e, b, fdim // tf),
        in_specs=[pl.BlockSpec((1, 1, c, d), lambda ei, bi, f: (bi, ei, 0, 0)),
                  pl.BlockSpec((1, 1, c, 1), lambda ei, bi, f: (bi, ei, 0, 0)),
                  pl.BlockSpec((1, d, tf), lambda ei, bi, f: (ei, 0, f)),
                  pl.BlockSpec((1, d, tf), lambda ei, bi, f: (ei, 0, f)),
                  pl.BlockSpec((1, tf, d), lambda ei, bi, f: (ei, f, 0))],
        out_specs=pl.BlockSpec((1, 1, c, d), lambda ei, bi, f: (bi, ei, 0, 0)),
        out_shape=jax.ShapeDtypeStruct((b, e, c, d), F32),
        scratch_shapes=[pltpu.VMEM((c, d), F32)],
        compiler_params=_cparams(("parallel", "parallel", "arbitrary")),
        name="ffn",
    )(xs, gate, w1, w3, w2)


def _final_kernel(h_ref, moe_ref, gt2_ref, g_ref, b_ref, o_ref):
    o_ref[0] = _ln(ALPHA * h_ref[0] + gt2_ref[0] * moe_ref[0], LN_EPS) * g_ref[...] + b_ref[...]


def _final_call(h1, moe, gt2, l2g, l2b):
    b, n, d = h1.shape
    tm = ROW_TILE
    row = pl.BlockSpec((1, tm, d), lambda bi, i: (bi, i, 0))
    vec = pl.BlockSpec((1, d), lambda bi, i: (0, 0))
    return pl.pallas_call(
        _final_kernel,
        grid=(b, n // tm),
        in_specs=[row, row, pl.BlockSpec((1, 1, d), lambda bi, i: (bi, 0, 0)), vec, vec],
        out_specs=row,
        out_shape=jax.ShapeDtypeStruct((b, n, d), F32),
        compiler_params=_cparams(("parallel", "parallel")),
        name="final",
    )(h1, moe, gt2, l2g, l2b)


def _rope_tables(n_ctx, n_lat):
    nf = HEAD_DIM // 4
    t = np.arange(n_lat)
    pos = np.stack([t // GRID_W, t % GRID_W], axis=-1).astype(np.float32)
    inv_freq = jnp.power(jnp.float32(ROPE_BASE), -jnp.arange(nf, dtype=F32) / nf)
    ang = jnp.asarray(pos)[:, :, None] * inv_freq
    cos = jnp.cos(ang)
    sin = jnp.sin(ang)
    cos_h = jnp.concatenate([cos[:, 0], cos[:, 0], cos[:, 1], cos[:, 1]], axis=-1)
    sin_h = jnp.concatenate([-sin[:, 0], sin[:, 0], -sin[:, 1], sin[:, 1]], axis=-1)
    cos_t = jnp.concatenate([jnp.ones((n_ctx, HEAD_DIM), F32), cos_h], axis=0)
    sin_t = jnp.concatenate([jnp.zeros((n_ctx, HEAD_DIM), F32), sin_h], axis=0)
    return jnp.tile(cos_t, (1, LANES // HEAD_DIM)), jnp.tile(sin_t, (1, LANES // HEAD_DIM))


def kernel(x, c, ctx, c_ctx, w_mod, b_mod, w_in, rpb, mu_prev, mu_next, w0, w_up, a0, a_up, g_up, k_k, k_a, r_k,
           gn_w, gn_b, w_pa, w_pr, w_o, ln1_g, ln1_b, w_router, w_e1, w_e3, w_e2, ln2_g, ln2_b):
    assert w_mod.shape[0] == DEPTH
    b, n, d = x.shape
    n_ctx = ctx.shape[1]
    assert n_ctx % ROW_TILE == 0 and n % ROW_TILE == 0 and (n // GRID_W) % NA_ROWS == 0
    t = n_ctx + n
    w = RW_WIDTH

    mod_rows = 8 * ((b + 1 + 7) // 8)
    cvec = jnp.zeros((mod_rows, d), F32).at[:b].set(c).at[b].set(c_ctx)
    mod = _mod_call(cvec, w_mod[0], b_mod[0][None])[:b + 1]
    sh1, sc1, gt1, sh2, sc2, gt2 = [mod[:, None, i * d:(i + 1) * d] for i in range(6)]

    cos_tab, sin_tab = _rope_tables(n_ctx, n)
    q_all, k_all, v_all, zrw, zgate, first_rows, last_rows = _proj_call(
        ctx, x, sh1, sc1, w_in[0].astype(BF16), cos_tab, sin_tab)

    y_na = _natten_call(q_all, k_all, v_all, _natten_bias_table(rpb[0]), n_ctx, n)

    nt = t // ROW_TILE
    ctx_tiles = n_ctx // ROW_TILE
    zero_row = jnp.zeros((b, 1, 1, RW_COLS), F32)
    prev_rows = jnp.concatenate([zero_row, last_rows[:, :nt - 1]], axis=1)
    next_rows = jnp.concatenate([first_rows[:, 1:], zero_row], axis=1)
    seq_start = (np.arange(nt) == 0) | (np.arange(nt) == ctx_tiles)
    seq_end = (np.arange(nt) == ctx_tiles - 1) | (np.arange(nt) == nt - 1)
    prev_rows = jnp.where(jnp.asarray(seq_start)[None, :, None, None], 0.0, prev_rows)
    next_rows = jnp.where(jnp.asarray(seq_end)[None, :, None, None], 0.0, next_rows)
    zpad = jnp.zeros((DECAY_LORA, w), F32)
    wup_pad = jnp.stack([jnp.concatenate([w_up[0, 0], zpad], 0), jnp.concatenate([zpad, w_up[0, 1]], 0)])
    aup_pad = jnp.stack([jnp.concatenate([a_up[0, 0], zpad], 0), jnp.concatenate([zpad, a_up[0, 1]], 0)])
    ones_bd = jnp.asarray(np.kron(np.eye(w // HEAD_DIM), np.ones((HEAD_DIM, HEAD_DIM))), BF16)
    r, v, kk, lwf, lwb, kmf, kmb, af, ab, g, bonus = _feat_call(
        zrw, prev_rows, next_rows, mu_prev, mu_next, k_k, k_a, r_k[0].reshape(1, w), w0[0][:, None, :], wup_pad,
        a0[0][:, None, :], aup_pad.astype(BF16), g_up[0].astype(BF16), ones_bd)

    g1, g2, q1, q2 = _prep_call(r, v, kk, lwf, lwb, kmf, kmb, af, ab)
    yf, yb = _scan_call(g1, g2, q1, q2, n_ctx // CHUNK)

    h1, u2, aff_t = _merge_call(yf, yb, bonus, g, zgate, y_na, x, gt1[:b], sh2[:b], sc2[:b],
                                w_pa[0].astype(BF16), w_pr[0].astype(BF16), w_o[0].astype(BF16),
                                gn_w, gn_b, ln1_g, ln1_b, jnp.transpose(w_router[0]), ones_bd, n_ctx)

    cap = CAPACITY_FACTOR * n // N_EXPERTS
    gate, idx = lax.top_k(aff_t, cap)
    flat_idx = (idx + (jnp.arange(b, dtype=jnp.int32) * n)[:, None, None]).reshape(-1)
    xs = jnp.take(u2.reshape(b * n, d), flat_idx, axis=0).reshape(b, N_EXPERTS, cap, d)
    ye = _ffn_call(xs, gate[..., None], w_e1[0], w_e3[0], w_e2[0])
    moe = jnp.zeros((b * n, d), F32).at[flat_idx].add(ye.reshape(-1, d)).reshape(b, n, d)

    return _final_call(h1, moe, gt2[:b], ln2_g, ln2_b)
```

```python
import functools
import math

import jax
import jax.numpy as jnp
import numpy as np
from jax import lax
from jax.experimental import pallas as pl
from jax.experimental.pallas import tpu as pltpu

F32 = jnp.float32
BF16 = jnp.bfloat16
HIGHEST = lax.Precision.HIGHEST

GRID_W = 64
NA_HEADS = 8
HEAD_DIM = 64
NA_WIDTH = NA_HEADS * HEAD_DIM
WIN_H = 8
WIN_W = 16
ROPE_BASE = 10000.0
RW_WIDTH = 512
DECAY_LORA = 64
AAA_LORA = 64
GATE_LORA = 128
RW_COLS = 3 * RW_WIDTH + 2 * DECAY_LORA + 2 * AAA_LORA + GATE_LORA
N_EXPERTS = 16
CAPACITY_FACTOR = 2
DEPTH = 1
ALPHA = (2.0 * DEPTH) ** 0.25
LN_EPS = 1e-6
GN_EPS = 64e-5
NEG_BIG = -1e30

LANES = 128
CHUNK = 64
ROW_TILE = 256
NA_ROWS = 4
FFN_TILE = 256
VMEM_LIMIT = 48 * 1024 * 1024


def _cparams(sem):
    return pltpu.CompilerParams(dimension_semantics=sem, vmem_limit_bytes=VMEM_LIMIT)


def _dot(a, b, prec=None):
    return jnp.dot(a, b, preferred_element_type=F32, precision=prec)


def _dot_nt(a, b, prec=None):
    return lax.dot_general(a, b, (((1,), (1,)), ((), ())), preferred_element_type=F32, precision=prec)


def _dot_split3(a, b):
    a_hi = a.astype(BF16)
    a_lo = (a - a_hi.astype(F32)).astype(BF16)
    b_hi = b.astype(BF16)
    b_lo = (b - b_hi.astype(F32)).astype(BF16)
    m = a.shape[0]
    hh_lh = _dot(jnp.concatenate([a_hi, a_lo], axis=0), b_hi)
    return hh_lh[0:m] + hh_lh[m:2 * m] + _dot(a_hi, b_lo)


def _ln(x, eps):
    mu = jnp.mean(x, axis=-1, keepdims=True)
    xc = x - mu
    var = jnp.mean(xc * xc, axis=-1, keepdims=True)
    return xc * lax.rsqrt(var + eps)


def _head_sum(x, ones_bd):
    hi = x.astype(BF16)
    lo = (x - hi.astype(F32)).astype(BF16)
    return _dot(hi, ones_bd) + _dot(lo, ones_bd)


def _mod_kernel(c_ref, w_ref, b_ref, o_ref):
    cv = c_ref[...]
    s = cv * jax.nn.sigmoid(cv)
    o_ref[...] = _dot(s, w_ref[...], HIGHEST) + b_ref[...]


def _mod_call(cvec, w_mod, b_mod):
    rows, d = cvec.shape
    n = w_mod.shape[1]
    tn = 1536
    return pl.pallas_call(
        _mod_kernel,
        grid=(n // tn,),
        in_specs=[pl.BlockSpec((rows, d), lambda j: (0, 0)),
                  pl.BlockSpec((d, tn), lambda j: (0, j)),
                  pl.BlockSpec((1, tn), lambda j: (0, j))],
        out_specs=pl.BlockSpec((rows, tn), lambda j: (0, j)),
        out_shape=jax.ShapeDtypeStruct((rows, n), F32),
        compiler_params=_cparams(("parallel",)),
        name="mod",
    )(cvec, w_mod, b_mod)


def _rope(z, cos, sin, x1_lane):
    outs = []
    for j in range(z.shape[1] // LANES):
        zj = z[:, j * LANES:(j + 1) * LANES]
        partner = jnp.where(x1_lane, pltpu.roll(zj, LANES - 16, 1), pltpu.roll(zj, 16, 1))
        outs.append(zj * cos + partner * sin)
    return jnp.concatenate(outs, axis=1)


def _proj_kernel(ctx_ref, x_ref, sh_ref, sc_ref, w_ref, cos_ref, sin_ref,
                 q_ref, k_ref, v_ref, rw_ref, g_ref, first_ref, last_ref, *, n_ctx_tiles):
    x = jnp.where(pl.program_id(1) < n_ctx_tiles, ctx_ref[0], x_ref[0])
    u = (_ln(x, LN_EPS) * (1.0 + sc_ref[0]) + sh_ref[0]).astype(BF16)
    cos = cos_ref[...]
    sin = sin_ref[...]
    lane = lax.broadcasted_iota(jnp.int32, cos.shape, 1)
    x1_lane = (lane % 32) < 16
    nw = NA_WIDTH
    zq = _dot(u, w_ref[:, 0:nw])
    q_ref[0] = (_rope(zq, cos, sin, x1_lane) * (HEAD_DIM ** -0.5)).astype(BF16)
    zk = _dot(u, w_ref[:, nw:2 * nw])
    k_ref[0] = _rope(zk, cos, sin, x1_lane).astype(BF16)
    v_ref[0] = _dot(u, w_ref[:, 2 * nw:3 * nw]).astype(BF16)
    zrw = _dot(u, w_ref[:, 3 * nw:3 * nw + RW_COLS])
    rw_ref[0] = zrw
    first_ref[0, 0] = zrw[0:1]
    last_ref[0, 0] = zrw[zrw.shape[0] - 1:]
    g_ref[0] = _dot(u, w_ref[:, 3 * nw + RW_COLS:]).astype(BF16)


def _proj_call(ctx, x, sh_tab, sc_tab, w_in_bf, cos_tab, sin_tab):
    b, n_ctx, d = ctx.shape
    t = n_ctx + x.shape[1]
    p_in = w_in_bf.shape[1]
    gate_cols = p_in - 3 * NA_WIDTH - RW_COLS
    tm = ROW_TILE
    n_ctx_tiles = n_ctx // tm
    nt = t // tm
    mod_idx = lambda bi, i: (jnp.where(i < n_ctx_tiles, b, bi), 0, 0)
    row_blk = lambda w: pl.BlockSpec((1, tm, w), lambda bi, i: (bi, i, 0))
    edge_blk = pl.BlockSpec((1, 1, 1, RW_COLS), lambda bi, i: (bi, i, 0, 0))
    edge_sd = jax.ShapeDtypeStruct((b, nt, 1, RW_COLS), F32)
    return pl.pallas_call(
        functools.partial(_proj_kernel, n_ctx_tiles=n_ctx_tiles),
        grid=(b, nt),
        in_specs=[pl.BlockSpec((1, tm, d), lambda bi, i: (bi, jnp.minimum(i, n_ctx_tiles - 1), 0)),
                  pl.BlockSpec((1, tm, d), lambda bi, i: (bi, jnp.maximum(i - n_ctx_tiles, 0), 0)),
                  pl.BlockSpec((1, 1, d), mod_idx),
                  pl.BlockSpec((1, 1, d), mod_idx),
                  pl.BlockSpec((d, p_in), lambda bi, i: (0, 0)),
                  pl.BlockSpec((tm, LANES), lambda bi, i: (i, 0)),
                  pl.BlockSpec((tm, LANES), lambda bi, i: (i, 0))],
        out_specs=[row_blk(NA_WIDTH), row_blk(NA_WIDTH), row_blk(NA_WIDTH), row_blk(RW_COLS), row_blk(gate_cols),
                   edge_blk, edge_blk],
        out_shape=[jax.ShapeDtypeStruct((b, t, NA_WIDTH), BF16)] * 3
        + [jax.ShapeDtypeStruct((b, t, RW_COLS), F32), jax.ShapeDtypeStruct((b, t, gate_cols), BF16),
           edge_sd, edge_sd],
        compiler_params=_cparams(("parallel", "parallel")),
        name="proj",
    )(ctx, x, sh_tab, sc_tab, w_in_bf, cos_tab, sin_tab)


def _natten_kernel(q_ref, k_ref, v_ref, bias_ref, o_ref, *, n_ctx, n_rows):
    r = pl.program_id(2)
    kc = k_ref[0, 0:n_ctx, :]
    vc = v_ref[0, 0:n_ctx, :]
    lane2 = lax.broadcasted_iota(jnp.int32, (GRID_W, LANES), 1)
    head0 = lane2 < HEAD_DIM
    rows = range(NA_ROWS)
    n2 = 2 * GRID_W
    q2, kw, vw, off = [], [], [], []
    for ii in rows:
        i = r * NA_ROWS + ii
        rs = jnp.clip(i - WIN_H // 2, 0, n_rows - WIN_H)
        off.append(rs - i + (WIN_H - 1))
        q = q_ref[0, ii * GRID_W:(ii + 1) * GRID_W, :]
        zero = jnp.zeros_like(q)
        q2.append(jnp.concatenate([jnp.where(head0, q, zero), jnp.where(head0, zero, q)], axis=0))
        start = pl.multiple_of(n_ctx + rs * GRID_W, GRID_W)
        kw.append(k_ref[0, pl.ds(start, WIN_H * GRID_W), :])
        vw.append(v_ref[0, pl.ds(start, WIN_H * GRID_W), :])
    s_ctx_all = _dot_nt(jnp.concatenate(q2, axis=0), kc)
    s_loc = [_dot_nt(q2[ii], kw[ii]) + bias_ref[0, off[ii]] for ii in rows]
    s_ctx = [s_ctx_all[ii * n2:(ii + 1) * n2] for ii in rows]
    m = [jnp.maximum(jnp.max(s_loc[ii], axis=-1, keepdims=True), jnp.max(s_ctx[ii], axis=-1, keepdims=True))
         for ii in rows]
    p_loc = [jnp.exp(s_loc[ii] - m[ii]) for ii in rows]
    p_ctx = [jnp.exp(s_ctx[ii] - m[ii]) for ii in rows]
    denom = [jnp.sum(p_loc[ii], axis=-1, keepdims=True) + jnp.sum(p_ctx[ii], axis=-1, keepdims=True) for ii in rows]
    o_ctx_all = _dot(jnp.concatenate(p_ctx, axis=0).astype(BF16), vc)
    o_loc = [_dot(p_loc[ii].astype(BF16), vw[ii]) for ii in rows]
    for ii in rows:
        o2 = (o_loc[ii] + o_ctx_all[ii * n2:(ii + 1) * n2]) / denom[ii]
        out = jnp.where(head0, o2[0:GRID_W], o2[GRID_W:n2])
        o_ref[0, ii * GRID_W:(ii + 1) * GRID_W, :] = out.astype(o_ref.dtype)


def _natten_call(q_all, k_all, v_all, bias2, n_ctx, n_lat):
    b, t, _ = q_all.shape
    n_rows = n_lat // GRID_W
    blk = NA_ROWS * GRID_W
    ctx_blocks = n_ctx // blk
    n_pairs = NA_HEADS // 2
    kv_spec = pl.BlockSpec((1, t, LANES), lambda bi, hp, r: (bi, 0, hp))
    return pl.pallas_call(
        functools.partial(_natten_kernel, n_ctx=n_ctx, n_rows=n_rows),
        grid=(b, n_pairs, n_rows // NA_ROWS),
        in_specs=[pl.BlockSpec((1, blk, LANES), lambda bi, hp, r: (bi, ctx_blocks + r, hp)),
                  kv_spec, kv_spec,
                  pl.BlockSpec((1, WIN_H, 2 * GRID_W, WIN_H * GRID_W), lambda bi, hp, r: (hp, 0, 0, 0))],
        out_specs=pl.BlockSpec((1, blk, LANES), lambda bi, hp, r: (bi, r, hp)),
        out_shape=jax.ShapeDtypeStruct((b, n_lat, NA_WIDTH), BF16),
        compiler_params=_cparams(("parallel", "parallel", "arbitrary")),
        name="natten",
    )(q_all, k_all, v_all, bias2)


def _natten_bias_table(rpb):
    col = np.arange(GRID_W)
    col_start = np.clip(col - WIN_W // 2, 0, GRID_W - WIN_W)
    in_win = (col[None, :] >= col_start[:, None]) & (col[None, :] < col_start[:, None] + WIN_W)
    col_off = np.clip(col[None, :] - col[:, None] + (WIN_W - 1), 0, 2 * WIN_W - 2)
    row_off = np.arange(WIN_H)[:, None] + np.arange(WIN_H)[None, :]
    tab = rpb[:, row_off][:, :, :, col_off]
    tab = jnp.where(jnp.asarray(in_win)[None, None, None], tab, NEG_BIG)
    tab = jnp.transpose(tab, (0, 1, 3, 2, 4))
    h = rpb.shape[0]
    tab = tab.reshape(h // 2, 2, WIN_H, GRID_W, WIN_H * GRID_W)
    tab = jnp.transpose(tab, (0, 2, 1, 3, 4)).reshape(h // 2, WIN_H, 2 * GRID_W, WIN_H * GRID_W)
    return tab.astype(F32)


def _feat_kernel(z_ref, prev_ref, next_ref, mup_ref, mun_ref, kk_ref, ka_ref, rk_ref, w0_ref, wup_ref, a0_ref,
                 aup_ref, gup_ref, ones_ref,
                 r_o, v_o, kk_o, lwf_o, lwb_o, kmf_o, kmb_o, af_o, ab_o, g_o, bonus_o):
    z = z_ref[0]
    tm = z.shape[0]
    row = lax.broadcasted_iota(jnp.int32, (tm, 1), 0)
    zp = jnp.where(row == 0, prev_ref[0, 0], pltpu.roll(z, 1, 0))
    zn = jnp.where(row == tm - 1, next_ref[0, 0], pltpu.roll(z, tm - 1, 0))
    zs = z + mup_ref[...] * (zp - z) + mun_ref[...] * (zn - z)
    w = RW_WIDTH
    r = zs[:, 0:w]
    k = zs[:, w:2 * w]
    v = zs[:, 2 * w:3 * w]
    wd = jnp.tanh(zs[:, 3 * w:3 * w + 2 * DECAY_LORA])
    ad = zs[:, 3 * w + 2 * DECAY_LORA:3 * w + 2 * DECAY_LORA + 2 * AAA_LORA].astype(BF16)
    gd = jax.nn.sigmoid(zs[:, 3 * w + 2 * DECAY_LORA + 2 * AAA_LORA:]).astype(BF16)
    ones_bd = ones_ref[...]
    kkf = k * kk_ref[...]
    kk = kkf * lax.rsqrt(_head_sum(kkf * kkf, ones_bd) + 1e-12)
    r_o[0] = r.astype(BF16)
    v_o[0] = v.astype(BF16)
    kk_o[0] = kk.astype(BF16)
    g_o[0] = _dot(gd, gup_ref[...]).astype(BF16)
    rk = r * rk_ref[...]
    acc = jnp.zeros_like(r)
    for d, (lw_o, km_o, a_o) in enumerate(((lwf_o, kmf_o, af_o), (lwb_o, kmb_o, ab_o))):
        w_raw = w0_ref[d] + _dot(wd, wup_ref[d], HIGHEST)
        lw_o[0] = -math.exp(-0.5) * jax.nn.sigmoid(w_raw)
        a = jax.nn.sigmoid(a0_ref[d] + _dot(ad, aup_ref[d]))
        kmod = k * (1.0 + (a - 1.0) * ka_ref[...])
        a_o[0] = a.astype(BF16)
        km_o[0] = kmod.astype(BF16)
        acc = acc + rk * kmod
    bonus_o[0] = (_head_sum(acc, ones_bd) * v).astype(BF16)


def _feat_call(zrw, prev_rows, next_rows, mu_prev, mu_next, k_k, k_a, r_k, w0, wup_pad, a0, aup_pad, g_up, ones_bd):
    b, t, c = zrw.shape
    tm = ROW_TILE
    w = RW_WIDTH
    full = lambda shape: pl.BlockSpec(shape, lambda bi, i: (0,) * len(shape))
    row = pl.BlockSpec((1, tm, w), lambda bi, i: (bi, i, 0))
    edge = pl.BlockSpec((1, 1, 1, c), lambda bi, i: (bi, i, 0, 0))
    return pl.pallas_call(
        _feat_kernel,
        grid=(b, t // tm),
        in_specs=[pl.BlockSpec((1, tm, c), lambda bi, i: (bi, i, 0)), edge, edge,
                  full((1, c)), full((1, c)), full((1, w)), full((1, w)), full((1, w)),
                  full((2, 1, w)), full((2, 2 * DECAY_LORA, w)), full((2, 1, w)), full((2, 2 * AAA_LORA, w)),
                  full((GATE_LORA, w)), full((w, w))],
        out_specs=[row] * 11,
        out_shape=[jax.ShapeDtypeStruct((b, t, w), dt) for dt in (BF16, BF16, BF16, F32, F32) + (BF16,) * 6],
        compiler_params=_cparams(("parallel", "parallel")),
        name="feat",
    )(zrw, prev_rows, next_rows, mu_prev, mu_next, k_k, k_a, r_k, w0, wup_pad, a0, aup_pad, g_up, ones_bd)


def _stack_bd(x, head0):
    zero = jnp.zeros_like(x)
    return jnp.concatenate([jnp.where(head0, x, zero), jnp.where(head0, zero, x)], axis=0)


def _unstack_bd(x):
    c = x.shape[0] // 2
    return x[0:c] + x[c:2 * c]


def _prep_kernel(r_ref, v_ref, kk_ref, lwf_ref, lwb_ref, kmf_ref, kmb_ref, af_ref, ab_ref,
                 g1_o, g2_o, q1_o, q2_o):
    c = CHUNK
    r = r_ref[0].astype(F32)
    v = v_ref[0].astype(F32)
    kk = kk_ref[0].astype(F32)
    n_pairs = r.shape[1] // LANES
    lane = lax.broadcasted_iota(jnp.int32, (c, LANES), 1)
    head0 = lane < HEAD_DIM
    ti = lax.broadcasted_iota(jnp.int32, (c, c), 0)
    si = lax.broadcasted_iota(jnp.int32, (c, c), 1)
    t2 = lax.broadcasted_iota(jnp.int32, (c, LANES), 0)
    s2 = lane % c
    eye = s2 == t2
    levels = int(math.log2(c))
    pair = lambda x, p: x[:, p * LANES:(p + 1) * LANES]
    bd16 = lambda x: _stack_bd(x, head0).astype(BF16)
    v_pairs = [bd16(pair(v, p)) for p in range(n_pairs)]
    wc, a_c, r_c, b_c, k_c, v_bd, past, incl, dest = [], [], [], [], [], [], [], [], []
    for d, (lw_ref, km_ref, a_ref) in enumerate(((lwf_ref, kmf_ref, af_ref), (lwb_ref, kmb_ref, ab_ref))):
        lw = lw_ref[0]
        if d == 0:
            tri = (si <= ti).astype(F32)
            past_d = s2 < t2
            last = c - 1
        else:
            tri = (si >= ti).astype(F32)
            past_d = s2 > t2
            last = 0
        incl_d = past_d | eye
        cs = _dot(tri, lw, HIGHEST)
        e_pos = jnp.exp(cs)
        e_neg = jnp.exp(-cs)
        a_all = -kk * jnp.exp(cs - lw)
        b_all = kk * a_ref[0].astype(F32) * e_neg
        k_all = km_ref[0].astype(F32) * e_neg
        r_all = r * e_pos
        wc_all = e_pos[last:last + 1, :]
        for p in range(n_pairs):
            wc.append(pair(wc_all, p))
            a_c.append(pair(a_all, p))
            r_c.append(pair(r_all, p))
            b_c.append(pair(b_all, p))
            k_c.append(pair(k_all, p))
            v_bd.append(v_pairs[p])
            past.append(past_d)
            incl.append(incl_d)
            dest.append((d, p))
    cs_ = range(len(dest))
    w = LANES
    b_bd = [_stack_bd(b_c[i], head0) for i in cs_]
    k_bd = [_stack_bd(k_c[i], head0) for i in cs_]
    gm = [_dot_nt(jnp.concatenate([a_c[i], r_c[i]], axis=0).astype(BF16),
                  jnp.concatenate([b_bd[i], k_bd[i]], axis=0).astype(BF16)) for i in cs_]
    aab = [jnp.where(past[i], gm[i][0:c, 0:w], 0.0) for i in cs_]
    aak = [jnp.where(past[i], gm[i][0:c, w:2 * w], 0.0) for i in cs_]
    mrb = [jnp.where(incl[i], gm[i][c:2 * c, 0:w], 0.0) for i in cs_]
    mrk = [jnp.where(incl[i], gm[i][c:2 * c, w:2 * w], 0.0) for i in cs_]
    bw_t = [_unstack_bd(jnp.transpose(b_bd[i] * wc[i])) for i in cs_]
    kw_t = [_unstack_bd(jnp.transpose(k_bd[i] * wc[i])) for i in cs_]
    xv = [_dot(jnp.concatenate([aak[i], mrk[i], kw_t[i]], axis=0).astype(BF16), v_bd[i]) for i in cs_]
    tinv = [jnp.where(eye, 1.0, aab[i]) for i in cs_]
    xp = [_dot(aab[i].astype(BF16), bd16(aab[i])) for i in cs_]
    for _ in range(2, levels):
        res = [_dot(jnp.concatenate([xp[i], tinv[i]], axis=0).astype(BF16), bd16(xp[i])) for i in cs_]
        xp = [res[i][0:c] for i in cs_]
        tinv = [tinv[i] + res[i][c:2 * c] for i in cs_]
    tinv = [tinv[i] + _dot(tinv[i].astype(BF16), bd16(xp[i])) for i in cs_]
    z = [_dot(tinv[i].astype(BF16), jnp.concatenate([bd16(a_c[i]), bd16(xv[i][0:c])], axis=1)) for i in cs_]
    z_bd = [jnp.concatenate([bd16(z[i][:, 0:w]), bd16(z[i][:, w:2 * w])], axis=1) for i in cs_]
    qg = [_dot(jnp.concatenate([mrb[i], bw_t[i]], axis=0).astype(BF16), z_bd[i]) for i in cs_]
    for i, (d, p) in enumerate(dest):
        q1_o[d, 0, p] = r_c[i] + qg[i][0:c, 0:w]
        q2_o[d, 0, p] = qg[i][0:c, w:2 * w] + xv[i][c:2 * c]
        g1_o[d, 0, p] = jnp.where(eye, wc[i], 0.0) + qg[i][c:2 * c, 0:w]
        g2_o[d, 0, p] = qg[i][c:2 * c, w:2 * w] + xv[i][2 * c:3 * c]


def _prep_call(r, v, kk, lwf, lwb, kmf, kmb, af, ab):
    b, t, w = r.shape
    n_pairs = w // LANES
    nch = t // CHUNK
    in_spec = pl.BlockSpec((1, CHUNK, w), lambda bi, j: (bi, j, 0))
    out_spec = pl.BlockSpec((2, 1, n_pairs, CHUNK, LANES), lambda bi, j: (0, j, bi, 0, 0))
    out_sd = jax.ShapeDtypeStruct((2, nch, b * n_pairs, CHUNK, LANES), F32)
    return pl.pallas_call(
        _prep_kernel,
        grid=(b, nch),
        in_specs=[in_spec] * 9,
        out_specs=[out_spec] * 4,
        out_shape=[out_sd] * 4,
        compiler_params=_cparams(("parallel", "parallel")),
        name="prep",
    )(r, v, kk, lwf, lwb, kmf, kmb, af, ab)


def _scan_kernel(g1f, g2f, q1f, q2f, g1b, g2b, q1b, q2b, yf_o, yb_o, hf_s, hb_s):
    @pl.when(pl.program_id(0) == 0)
    def _():
        hf_s[...] = jnp.zeros_like(hf_s)
        hb_s[...] = jnp.zeros_like(hb_s)

    lane = lax.broadcasted_iota(jnp.int32, (CHUNK, LANES), 1)
    head0 = lane < HEAD_DIM
    n_pairs = hf_s.shape[0]
    c = CHUNK
    for g1, g2, q1, q2, y_o, h_s in ((g1f, g2f, q1f, q2f, yf_o, hf_s), (g1b, g2b, q1b, q2b, yb_o, hb_s)):
        for p in range(n_pairs):
            lhs = jnp.concatenate([g1[0, 0, p], q1[0, 0, p]], axis=0)
            res = _dot_split3(lhs, _stack_bd(h_s[p], head0))
            h_s[p] = res[0:c] + g2[0, 0, p]
            y_o[0, p] = res[c:2 * c] + q2[0, 0, p]


def _scan_call(g1, g2, q1, q2, n_ctx_chunks):
    _, nch, npair, c, lanes = g1.shape
    n_lat_chunks = nch - n_ctx_chunks
    fwd = lambda s: (0, s, 0, 0, 0)
    bwd_chunk = lambda s: jnp.where(s < n_ctx_chunks, n_ctx_chunks - 1 - s, 2 * n_ctx_chunks + n_lat_chunks - 1 - s)
    bwd = lambda s: (1, bwd_chunk(s), 0, 0, 0)
    blk = (1, 1, npair, c, lanes)
    y_sd = jax.ShapeDtypeStruct((nch, npair, c, lanes), F32)
    return pl.pallas_call(
        _scan_kernel,
        grid=(nch,),
        in_specs=[pl.BlockSpec(blk, fwd)] * 4 + [pl.BlockSpec(blk, bwd)] * 4,
        out_specs=[pl.BlockSpec((1, npair, c, lanes), lambda s: (s, 0, 0, 0)),
                   pl.BlockSpec((1, npair, c, lanes), lambda s: (bwd_chunk(s), 0, 0, 0))],
        out_shape=[y_sd, y_sd],
        scratch_shapes=[pltpu.VMEM((npair, c, lanes), F32), pltpu.VMEM((npair, c, lanes), F32)],
        compiler_params=_cparams(("arbitrary",)),
        name="scan",
    )(g1, g2, q1, q2, g1, g2, q1, q2)


def _merge_kernel(yf_ref, yb_ref, bonus_ref, g_ref, zg_ref, yna_ref, x_ref, gt1_ref, sh2_ref, sc2_ref,
                  wpa_ref, wpr_ref, wo_ref, gnw_ref, gnb_ref, l1g_ref, l1b_ref, wrt_ref, ones_ref,
                  h1_o, u2_o, aff_o):
    tm = x_ref.shape[1]
    n_pairs = yf_ref.shape[1]
    ys = yf_ref[...] + yb_ref[...]
    y = jnp.concatenate([ys[:, p].reshape(tm, LANES) for p in range(n_pairs)], axis=1)
    ones_bd = ones_ref[...]
    inv = 1.0 / HEAD_DIM
    mu = _head_sum(y, ones_bd) * inv
    yc = y - mu
    var = _head_sum(yc * yc, ones_bd) * inv
    yn = yc * lax.rsqrt(var + GN_EPS) * gnw_ref[...] + gnb_ref[...]
    y_rw = (yn + bonus_ref[0].astype(F32)) * g_ref[0].astype(F32)
    gates = jax.nn.sigmoid(zg_ref[0].astype(F32))
    d = x_ref.shape[2]
    m1 = gates[:, 0:d] * _dot(yna_ref[0], wpa_ref[...]) + gates[:, d:2 * d] * _dot(y_rw.astype(BF16), wpr_ref[...])
    m = _dot(m1.astype(BF16), wo_ref[...])
    h1 = _ln(ALPHA * x_ref[0] + gt1_ref[0] * m, LN_EPS) * l1g_ref[...] + l1b_ref[...]
    h1_o[0] = h1
    u2 = _ln(h1, LN_EPS) * (1.0 + sc2_ref[0]) + sh2_ref[0]
    u2_o[0] = u2.astype(BF16)
    logits = _dot_nt(wrt_ref[...], u2, HIGHEST)
    mx = jnp.max(logits, axis=0, keepdims=True)
    e = jnp.exp(logits - mx)
    aff_o[0] = e / jnp.sum(e, axis=0, keepdims=True)


def _merge_call(yf, yb, bonus, g, zgate, y_na, x, gt1, sh2, sc2, w_pa, w_pr, w_o, gn_w, gn_b, l1g, l1b, w_rt,
                ones_bd, n_ctx):
    b, n, d = x.shape
    tm = ROW_TILE
    cpt = tm // CHUNK
    n_pairs = RW_WIDTH // LANES
    ctx_tiles = n_ctx // tm
    e = w_rt.shape[0]
    y_spec = pl.BlockSpec((cpt, n_pairs, CHUNK, LANES), lambda bi, i: (ctx_tiles + i, bi, 0, 0))
    full = lambda shape: pl.BlockSpec(shape, lambda bi, i: (0,) * len(shape))
    modv = pl.BlockSpec((1, 1, d), lambda bi, i: (bi, 0, 0))
    return pl.pallas_call(
        _merge_kernel,
        grid=(b, n // tm),
        in_specs=[y_spec, y_spec,
                  pl.BlockSpec((1, tm, RW_WIDTH), lambda bi, i: (bi, ctx_tiles + i, 0)),
                  pl.BlockSpec((1, tm, RW_WIDTH), lambda bi, i: (bi, ctx_tiles + i, 0)),
                  pl.BlockSpec((1, tm, 2 * d), lambda bi, i: (bi, ctx_tiles + i, 0)),
                  pl.BlockSpec((1, tm, NA_WIDTH), lambda bi, i: (bi, i, 0)),
                  pl.BlockSpec((1, tm, d), lambda bi, i: (bi, i, 0)),
                  modv, modv, modv,
                  full((NA_WIDTH, d)), full((RW_WIDTH, d)), full((d, d)),
                  full((1, RW_WIDTH)), full((1, RW_WIDTH)), full((1, d)), full((1, d)), full((e, d)),
                  full((RW_WIDTH, RW_WIDTH))],
        out_specs=[pl.BlockSpec((1, tm, d), lambda bi, i: (bi, i, 0)),
                   pl.BlockSpec((1, tm, d), lambda bi, i: (bi, i, 0)),
                   pl.BlockSpec((1, e, tm), lambda bi, i: (bi, 0, i))],
        out_shape=[jax.ShapeDtypeStruct((b, n, d), F32), jax.ShapeDtypeStruct((b, n, d), BF16),
                   jax.ShapeDtypeStruct((b, e, n), F32)],
        compiler_params=_cparams(("parallel", "parallel")),
        name="merge",
    )(yf, yb, bonus, g, zgate, y_na, x, gt1, sh2, sc2, w_pa, w_pr, w_o, gn_w, gn_b, l1g, l1b, w_rt, ones_bd)


def _ffn_kernel(x_ref, gate_ref, w1_ref, w3_ref, w2_ref, o_ref, acc_ref):
    f = pl.program_id(2)

    @pl.when(f == 0)
    def _():
        acc_ref[...] = jnp.zeros_like(acc_ref)

    xs = x_ref[0, 0]
    a = _dot(xs, w1_ref[0].astype(BF16))
    bb = _dot(xs, w3_ref[0].astype(BF16))
    hdn = (a * jax.nn.sigmoid(a) * bb).astype(BF16)
    acc_ref[...] += _dot(hdn, w2_ref[0].astype(BF16))

    @pl.when(f == pl.num_programs(2) - 1)
    def _():
        o_ref[0, 0] = acc_ref[...] * gate_ref[0, 0]


def _ffn_call(xs, gate, w1, w3, w2):
    b, e, c, d = xs.shape
    fdim = w1.shape[2]
    tf = FFN_TILE
    return pl.pallas_call(
        _ffn_kernel,
        grid=(e, b, fdim // tf),
        in_specs=[pl.BlockSpec((1, 1, c, d), lambda ei, bi, f: (bi, ei, 0, 0)),
                  pl.BlockSpec((1, 1, c, 1), lambda ei, bi, f: (bi, ei, 0, 0)),
                  pl.BlockSpec((1, d, tf), lambda ei, bi, f: (ei, 0, f)),
                  pl.BlockSpec((1, d, tf), lambda ei, bi, f: (ei, 0, f)),
                  pl.BlockSpec((1, tf, d), lambda ei, bi, f: (ei, f, 0))],
        out_specs=pl.BlockSpec((1, 1, c, d), lambda ei, bi, f: (bi, ei, 0, 0)),
        out_shape=jax.ShapeDtypeStruct((b, e, c, d), F32),
        scratch_shapes=[pltpu.VMEM((c, d), F32)],
        compiler_params=_cparams(("parallel", "parallel", "arbitrary")),
        name="ffn",
    )(xs, gate, w1, w3, w2)


def _final_kernel(h_ref, moe_ref, gt2_ref, g_ref, b_ref, o_ref):
    o_ref[0] = _ln(ALPHA * h_ref[0] + gt2_ref[0] * moe_ref[0], LN_EPS) * g_ref[...] + b_ref[...]


def _final_call(h1, moe, gt2, l2g, l2b):
    b, n, d = h1.shape
    tm = ROW_TILE
    row = pl.BlockSpec((1, tm, d), lambda bi, i: (bi, i, 0))
    vec = pl.BlockSpec((1, d), lambda bi, i: (0, 0))
    return pl.pallas_call(
        _final_kernel,
        grid=(b, n // tm),
        in_specs=[row, row, pl.BlockSpec((1, 1, d), lambda bi, i: (bi, 0, 0)), vec, vec],
        out_specs=row,
        out_shape=jax.ShapeDtypeStruct((b, n, d), F32),
        compiler_params=_cparams(("parallel", "parallel")),
        name="final",
    )(h1, moe, gt2, l2g, l2b)


def _rope_tables(n_ctx, n_lat):
    nf = HEAD_DIM // 4
    t = np.arange(n_lat)
    pos = np.stack([t // GRID_W, t % GRID_W], axis=-1).astype(np.float32)
    inv_freq = jnp.power(jnp.float32(ROPE_BASE), -jnp.arange(nf, dtype=F32) / nf)
    ang = jnp.asarray(pos)[:, :, None] * inv_freq
    cos = jnp.cos(ang)
    sin = jnp.sin(ang)
    cos_h = jnp.concatenate([cos[:, 0], cos[:, 0], cos[:, 1], cos[:, 1]], axis=-1)
    sin_h = jnp.concatenate([-sin[:, 0], sin[:, 0], -sin[:, 1], sin[:, 1]], axis=-1)
    cos_t = jnp.concatenate([jnp.ones((n_ctx, HEAD_DIM), F32), cos_h], axis=0)
    sin_t = jnp.concatenate([jnp.zeros((n_ctx, HEAD_DIM), F32), sin_h], axis=0)
    return jnp.tile(cos_t, (1, LANES // HEAD_DIM)), jnp.tile(sin_t, (1, LANES // HEAD_DIM))


def kernel(x, c, ctx, c_ctx, w_mod, b_mod, w_in, rpb, mu_prev, mu_next, w0, w_up, a0, a_up, g_up, k_k, k_a, r_k,
           gn_w, gn_b, w_pa, w_pr, w_o, ln1_g, ln1_b, w_router, w_e1, w_e3, w_e2, ln2_g, ln2_b):
    assert w_mod.shape[0] == DEPTH
    b, n, d = x.shape
    n_ctx = ctx.shape[1]
    assert n_ctx % ROW_TILE == 0 and n % ROW_TILE == 0 and (n // GRID_W) % NA_ROWS == 0
    t = n_ctx + n
    w = RW_WIDTH

    mod_rows = 8 * ((b + 1 + 7) // 8)
    cvec = jnp.zeros((mod_rows, d), F32).at[:b].set(c).at[b].set(c_ctx)
    mod = _mod_call(cvec, w_mod[0], b_mod[0][None])[:b + 1]
    sh1, sc1, gt1, sh2, sc2, gt2 = [mod[:, None, i * d:(i + 1) * d] for i in range(6)]

    cos_tab, sin_tab = _rope_tables(n_ctx, n)
    q_all, k_all, v_all, zrw, zgate, first_rows, last_rows = _proj_call(
        ctx, x, sh1, sc1, w_in[0].astype(BF16), cos_tab, sin_tab)

    y_na = _natten_call(q_all, k_all, v_all, _natten_bias_table(rpb[0]), n_ctx, n)

    nt = t // ROW_TILE
    ctx_tiles = n_ctx // ROW_TILE
    zero_row = jnp.zeros((b, 1, 1, RW_COLS), F32)
    prev_rows = jnp.concatenate([zero_row, last_rows[:, :nt - 1]], axis=1)
    next_rows = jnp.concatenate([first_rows[:, 1:], zero_row], axis=1)
    seq_start = (np.arange(nt) == 0) | (np.arange(nt) == ctx_tiles)
    seq_end = (np.arange(nt) == ctx_tiles - 1) | (np.arange(nt) == nt - 1)
    prev_rows = jnp.where(jnp.asarray(seq_start)[None, :, None, None], 0.0, prev_rows)
    next_rows = jnp.where(jnp.asarray(seq_end)[None, :, None, None], 0.0, next_rows)
    zpad = jnp.zeros((DECAY_LORA, w), F32)
    wup_pad = jnp.stack([jnp.concatenate([w_up[0, 0], zpad], 0), jnp.concatenate([zpad, w_up[0, 1]], 0)])
    aup_pad = jnp.stack([jnp.concatenate([a_up[0, 0], zpad], 0), jnp.concatenate([zpad, a_up[0, 1]], 0)])
    ones_bd = jnp.asarray(np.kron(np.eye(w // HEAD_DIM), np.ones((HEAD_DIM, HEAD_DIM))), BF16)
    r, v, kk, lwf, lwb, kmf, kmb, af, ab, g, bonus = _feat_call(
        zrw, prev_rows, next_rows, mu_prev, mu_next, k_k, k_a, r_k[0].reshape(1, w), w0[0][:, None, :], wup_pad,
        a0[0][:, None, :], aup_pad.astype(BF16), g_up[0].astype(BF16), ones_bd)

    g1, g2, q1, q2 = _prep_call(r, v, kk, lwf, lwb, kmf, kmb, af, ab)
    yf, yb = _scan_call(g1, g2, q1, q2, n_ctx // CHUNK)

    h1, u2, aff_t = _merge_call(yf, yb, bonus, g, zgate, y_na, x, gt1[:b], sh2[:b], sc2[:b],
                                w_pa[0].astype(BF16), w_pr[0].astype(BF16), w_o[0].astype(BF16),
                                gn_w, gn_b, ln1_g, ln1_b, jnp.transpose(w_router[0]), ones_bd, n_ctx)

    cap = CAPACITY_FACTOR * n // N_EXPERTS
    gate, idx = lax.top_k(aff_t, cap)
    flat_idx = (idx + (jnp.arange(b, dtype=jnp.int32) * n)[:, None, None]).reshape(-1)
    xs = jnp.take(u2.reshape(b * n, d), flat_idx, axis=0).reshape(b, N_EXPERTS, cap, d)
    ye = _ffn_call(xs, gate[..., None], w_e1[0], w_e3[0], w_e2[0])
    moe = jnp.zeros((b * n, d), F32).at[flat_idx].add(ye.reshape(-1, d)).reshape(b, n, d)

    return _final_call(h1, moe, gt2[:b], ln2_g, ln2_b)
```

```python
import functools
import math

import jax
import jax.numpy as jnp
import numpy as np
from jax import lax
from jax.experimental import pallas as pl
from jax.experimental.pallas import tpu as pltpu

F32 = jnp.float32
BF16 = jnp.bfloat16
HIGHEST = lax.Precision.HIGHEST

GRID_W = 64
NA_HEADS = 8
HEAD_DIM = 64
NA_WIDTH = NA_HEADS * HEAD_DIM
WIN_H = 8
WIN_W = 16
ROPE_BASE = 10000.0
RW_WIDTH = 512
DECAY_LORA = 64
AAA_LORA = 64
GATE_LORA = 128
RW_COLS = 3 * RW_WIDTH + 2 * DECAY_LORA + 2 * AAA_LORA + GATE_LORA
N_EXPERTS = 16
CAPACITY_FACTOR = 2
DEPTH = 1
ALPHA = (2.0 * DEPTH) ** 0.25
LN_EPS = 1e-6
GN_EPS = 64e-5
NEG_BIG = -1e30

LANES = 128
CHUNK = 64
PREP_CHUNKS = 2
ROW_TILE = 256
NA_ROWS = 4
FFN_TILE = 256
VMEM_LIMIT = 48 * 1024 * 1024


def _cparams(sem):
    return pltpu.CompilerParams(dimension_semantics=sem, vmem_limit_bytes=VMEM_LIMIT)


def _dot(a, b, prec=None):
    return jnp.dot(a, b, preferred_element_type=F32, precision=prec)


def _dot_nt(a, b, prec=None):
    return lax.dot_general(a, b, (((1,), (1,)), ((), ())), preferred_element_type=F32, precision=prec)


def _dot_split3(a, b):
    a_hi = a.astype(BF16)
    a_lo = (a - a_hi.astype(F32)).astype(BF16)
    b_hi = b.astype(BF16)
    b_lo = (b - b_hi.astype(F32)).astype(BF16)
    m = a.shape[0]
    hh_lh = _dot(jnp.concatenate([a_hi, a_lo], axis=0), b_hi)
    return hh_lh[0:m] + hh_lh[m:2 * m] + _dot(a_hi, b_lo)


def _ln(x, eps):
    mu = jnp.mean(x, axis=-1, keepdims=True)
    xc = x - mu
    var = jnp.mean(xc * xc, axis=-1, keepdims=True)
    return xc * lax.rsqrt(var + eps)


def _head_sum(x, ones_bd):
    hi = x.astype(BF16)
    lo = (x - hi.astype(F32)).astype(BF16)
    return _dot(hi, ones_bd) + _dot(lo, ones_bd)


def _mod_kernel(c_ref, w_ref, b_ref, o_ref):
    cv = c_ref[...]
    s = cv * jax.nn.sigmoid(cv)
    o_ref[...] = _dot(s, w_ref[...], HIGHEST) + b_ref[...]


def _mod_call(cvec, w_mod, b_mod):
    rows, d = cvec.shape
    n = w_mod.shape[1]
    tn = 1536
    return pl.pallas_call(
        _mod_kernel,
        grid=(n // tn,),
        in_specs=[pl.BlockSpec((rows, d), lambda j: (0, 0)),
                  pl.BlockSpec((d, tn), lambda j: (0, j)),
                  pl.BlockSpec((1, tn), lambda j: (0, j))],
        out_specs=pl.BlockSpec((rows, tn), lambda j: (0, j)),
        out_shape=jax.ShapeDtypeStruct((rows, n), F32),
        compiler_params=_cparams(("parallel",)),
        name="mod",
    )(cvec, w_mod, b_mod)


def _rope(z, cos, sin, x1_lane):
    outs = []
    for j in range(z.shape[1] // LANES):
        zj = z[:, j * LANES:(j + 1) * LANES]
        partner = jnp.where(x1_lane, pltpu.roll(zj, LANES - 16, 1), pltpu.roll(zj, 16, 1))
        outs.append(zj * cos + partner * sin)
    return jnp.concatenate(outs, axis=1)


def _proj_kernel(ctx_ref, x_ref, sh_ref, sc_ref, w_ref, cos_ref, sin_ref,
                 q_ref, k_ref, v_ref, rw_ref, g_ref, first_ref, last_ref, *, n_ctx_tiles):
    x = jnp.where(pl.program_id(1) < n_ctx_tiles, ctx_ref[0], x_ref[0])
    u = (_ln(x, LN_EPS) * (1.0 + sc_ref[0]) + sh_ref[0]).astype(BF16)
    cos = cos_ref[...]
    sin = sin_ref[...]
    lane = lax.broadcasted_iota(jnp.int32, cos.shape, 1)
    x1_lane = (lane % 32) < 16
    nw = NA_WIDTH
    zq = _dot(u, w_ref[:, 0:nw])
    q_ref[0] = (_rope(zq, cos, sin, x1_lane) * (HEAD_DIM ** -0.5)).astype(BF16)
    zk = _dot(u, w_ref[:, nw:2 * nw])
    k_ref[0] = _rope(zk, cos, sin, x1_lane).astype(BF16)
    v_ref[0] = _dot(u, w_ref[:, 2 * nw:3 * nw]).astype(BF16)
    zrw = _dot(u, w_ref[:, 3 * nw:3 * nw + RW_COLS])
    rw_ref[0] = zrw
    first_ref[0, 0] = zrw[0:1]
    last_ref[0, 0] = zrw[zrw.shape[0] - 1:]
    g_ref[0] = _dot(u, w_ref[:, 3 * nw + RW_COLS:]).astype(BF16)


def _proj_call(ctx, x, sh_tab, sc_tab, w_in_bf, cos_tab, sin_tab):
    b, n_ctx, d = ctx.shape
    t = n_ctx + x.shape[1]
    p_in = w_in_bf.shape[1]
    gate_cols = p_in - 3 * NA_WIDTH - RW_COLS
    tm = ROW_TILE
    n_ctx_tiles = n_ctx // tm
    nt = t // tm
    mod_idx = lambda bi, i: (jnp.where(i < n_ctx_tiles, b, bi), 0, 0)
    row_blk = lambda w: pl.BlockSpec((1, tm, w), lambda bi, i: (bi, i, 0))
    edge_blk = pl.BlockSpec((1, 1, 1, RW_COLS), lambda bi, i: (bi, i, 0, 0))
    edge_sd = jax.ShapeDtypeStruct((b, nt, 1, RW_COLS), F32)
    return pl.pallas_call(
        functools.partial(_proj_kernel, n_ctx_tiles=n_ctx_tiles),
        grid=(b, nt),
        in_specs=[pl.BlockSpec((1, tm, d), lambda bi, i: (bi, jnp.minimum(i, n_ctx_tiles - 1), 0)),
                  pl.BlockSpec((1, tm, d), lambda bi, i: (bi, jnp.maximum(i - n_ctx_tiles, 0), 0)),
                  pl.BlockSpec((1, 1, d), mod_idx),
                  pl.BlockSpec((1, 1, d), mod_idx),
                  pl.BlockSpec((d, p_in), lambda bi, i: (0, 0)),
                  pl.BlockSpec((tm, LANES), lambda bi, i: (i, 0)),
                  pl.BlockSpec((tm, LANES), lambda bi, i: (i, 0))],
        out_specs=[row_blk(NA_WIDTH), row_blk(NA_WIDTH), row_blk(NA_WIDTH), row_blk(RW_COLS), row_blk(gate_cols),
                   edge_blk, edge_blk],
        out_shape=[jax.ShapeDtypeStruct((b, t, NA_WIDTH), BF16)] * 3
        + [jax.ShapeDtypeStruct((b, t, RW_COLS), F32), jax.ShapeDtypeStruct((b, t, gate_cols), BF16),
           edge_sd, edge_sd],
        compiler_params=_cparams(("parallel", "parallel")),
        name="proj",
    )(ctx, x, sh_tab, sc_tab, w_in_bf, cos_tab, sin_tab)


def _natten_kernel(q_ref, k_ref, v_ref, bias_ref, o_ref, *, n_ctx, n_rows):
    r = pl.program_id(2)
    kc = k_ref[0, 0:n_ctx, :]
    vc = v_ref[0, 0:n_ctx, :]
    lane2 = lax.broadcasted_iota(jnp.int32, (GRID_W, LANES), 1)
    head0 = lane2 < HEAD_DIM
    rows = range(NA_ROWS)
    n2 = 2 * GRID_W
    q2, kw, vw, off = [], [], [], []
    for ii in rows:
        i = r * NA_ROWS + ii
        rs = jnp.clip(i - WIN_H // 2, 0, n_rows - WIN_H)
        off.append(rs - i + (WIN_H - 1))
        q = q_ref[0, ii * GRID_W:(ii + 1) * GRID_W, :]
        zero = jnp.zeros_like(q)
        q2.append(jnp.concatenate([jnp.where(head0, q, zero), jnp.where(head0, zero, q)], axis=0))
        start = pl.multiple_of(n_ctx + rs * GRID_W, GRID_W)
        kw.append(k_ref[0, pl.ds(start, WIN_H * GRID_W), :])
        vw.append(v_ref[0, pl.ds(start, WIN_H * GRID_W), :])
    s_ctx_all = _dot_nt(jnp.concatenate(q2, axis=0), kc)
    s_loc = [_dot_nt(q2[ii], kw[ii]) + bias_ref[0, off[ii]] for ii in rows]
    s_ctx = [s_ctx_all[ii * n2:(ii + 1) * n2] for ii in rows]
    m = [jnp.maximum(jnp.max(s_loc[ii], axis=-1, keepdims=True), jnp.max(s_ctx[ii], axis=-1, keepdims=True))
         for ii in rows]
    p_loc = [jnp.exp(s_loc[ii] - m[ii]) for ii in rows]
    p_ctx = [jnp.exp(s_ctx[ii] - m[ii]) for ii in rows]
    denom = [jnp.sum(p_loc[ii], axis=-1, keepdims=True) + jnp.sum(p_ctx[ii], axis=-1, keepdims=True) for ii in rows]
    o_ctx_all = _dot(jnp.concatenate(p_ctx, axis=0).astype(BF16), vc)
    o_loc = [_dot(p_loc[ii].astype(BF16), vw[ii]) for ii in rows]
    for ii in rows:
        o2 = (o_loc[ii] + o_ctx_all[ii * n2:(ii + 1) * n2]) / denom[ii]
        out = jnp.where(head0, o2[0:GRID_W], o2[GRID_W:n2])
        o_ref[0, ii * GRID_W:(ii + 1) * GRID_W, :] = out.astype(o_ref.dtype)


def _natten_call(q_all, k_all, v_all, bias2, n_ctx, n_lat):
    b, t, _ = q_all.shape
    n_rows = n_lat // GRID_W
    blk = NA_ROWS * GRID_W
    ctx_blocks = n_ctx // blk
    n_pairs = NA_HEADS // 2
    kv_spec = pl.BlockSpec((1, t, LANES), lambda bi, hp, r: (bi, 0, hp))
    return pl.pallas_call(
        functools.partial(_natten_kernel, n_ctx=n_ctx, n_rows=n_rows),
        grid=(b, n_pairs, n_rows // NA_ROWS),
        in_specs=[pl.BlockSpec((1, blk, LANES), lambda bi, hp, r: (bi, ctx_blocks + r, hp)),
                  kv_spec, kv_spec,
                  pl.BlockSpec((1, WIN_H, 2 * GRID_W, WIN_H * GRID_W), lambda bi, hp, r: (hp, 0, 0, 0))],
        out_specs=pl.BlockSpec((1, blk, LANES), lambda bi, hp, r: (bi, r, hp)),
        out_shape=jax.ShapeDtypeStruct((b, n_lat, NA_WIDTH), BF16),
        compiler_params=_cparams(("parallel", "parallel", "arbitrary")),
        name="natten",
    )(q_all, k_all, v_all, bias2)


def _natten_bias_table(rpb):
    col = np.arange(GRID_W)
    col_start = np.clip(col - WIN_W // 2, 0, GRID_W - WIN_W)
    in_win = (col[None, :] >= col_start[:, None]) & (col[None, :] < col_start[:, None] + WIN_W)
    col_off = np.clip(col[None, :] - col[:, None] + (WIN_W - 1), 0, 2 * WIN_W - 2)
    row_off = np.arange(WIN_H)[:, None] + np.arange(WIN_H)[None, :]
    tab = rpb[:, row_off][:, :, :, col_off]
    tab = jnp.where(jnp.asarray(in_win)[None, None, None], tab, NEG_BIG)
    tab = jnp.transpose(tab, (0, 1, 3, 2, 4))
    h = rpb.shape[0]
    tab = tab.reshape(h // 2, 2, WIN_H, GRID_W, WIN_H * GRID_W)
    tab = jnp.transpose(tab, (0, 2, 1, 3, 4)).reshape(h // 2, WIN_H, 2 * GRID_W, WIN_H * GRID_W)
    return tab.astype(F32)


def _feat_kernel(z_ref, prev_ref, next_ref, mup_ref, mun_ref, kk_ref, ka_ref, rk_ref, w0_ref, wup_ref, a0_ref,
                 aup_ref, gup_ref, ones_ref,
                 r_o, v_o, kk_o, lwf_o, lwb_o, kmf_o, kmb_o, af_o, ab_o, g_o, bonus_o):
    z = z_ref[0]
    tm = z.shape[0]
    row = lax.broadcasted_iota(jnp.int32, (tm, 1), 0)
    zp = jnp.where(row == 0, prev_ref[0, 0], pltpu.roll(z, 1, 0))
    zn = jnp.where(row == tm - 1, next_ref[0, 0], pltpu.roll(z, tm - 1, 0))
    zs = z + mup_ref[...] * (zp - z) + mun_ref[...] * (zn - z)
    w = RW_WIDTH
    r = zs[:, 0:w]
    k = zs[:, w:2 * w]
    v = zs[:, 2 * w:3 * w]
    wd = jnp.tanh(zs[:, 3 * w:3 * w + 2 * DECAY_LORA])
    ad = zs[:, 3 * w + 2 * DECAY_LORA:3 * w + 2 * DECAY_LORA + 2 * AAA_LORA].astype(BF16)
    gd = jax.nn.sigmoid(zs[:, 3 * w + 2 * DECAY_LORA + 2 * AAA_LORA:]).astype(BF16)
    ones_bd = ones_ref[...]
    kkf = k * kk_ref[...]
    kk = kkf * lax.rsqrt(_head_sum(kkf * kkf, ones_bd) + 1e-12)
    r_o[0] = r.astype(BF16)
    v_o[0] = v.astype(BF16)
    kk_o[0] = kk.astype(BF16)
    g_o[0] = _dot(gd, gup_ref[...]).astype(BF16)
    rk = r * rk_ref[...]
    acc = jnp.zeros_like(r)
    for d, (lw_o, km_o, a_o) in enumerate(((lwf_o, kmf_o, af_o), (lwb_o, kmb_o, ab_o))):
        w_raw = w0_ref[d] + _dot(wd, wup_ref[d], HIGHEST)
        lw_o[0] = -math.exp(-0.5) * jax.nn.sigmoid(w_raw)
        a = jax.nn.sigmoid(a0_ref[d] + _dot(ad, aup_ref[d]))
        kmod = k * (1.0 + (a - 1.0) * ka_ref[...])
        a_o[0] = a.astype(BF16)
        km_o[0] = kmod.astype(BF16)
        acc = acc + rk * kmod
    bonus_o[0] = (_head_sum(acc, ones_bd) * v).astype(BF16)


def _feat_call(zrw, prev_rows, next_rows, mu_prev, mu_next, k_k, k_a, r_k, w0, wup_pad, a0, aup_pad, g_up, ones_bd):
    b, t, c = zrw.shape
    tm = ROW_TILE
    w = RW_WIDTH
    full = lambda shape: pl.BlockSpec(shape, lambda bi, i: (0,) * len(shape))
    row = pl.BlockSpec((1, tm, w), lambda bi, i: (bi, i, 0))
    edge = pl.BlockSpec((1, 1, 1, c), lambda bi, i: (bi, i, 0, 0))
    return pl.pallas_call(
        _feat_kernel,
        grid=(b, t // tm),
        in_specs=[pl.BlockSpec((1, tm, c), lambda bi, i: (bi, i, 0)), edge, edge,
                  full((1, c)), full((1, c)), full((1, w)), full((1, w)), full((1, w)),
                  full((2, 1, w)), full((2, 2 * DECAY_LORA, w)), full((2, 1, w)), full((2, 2 * AAA_LORA, w)),
                  full((GATE_LORA, w)), full((w, w))],
        out_specs=[row] * 11,
        out_shape=[jax.ShapeDtypeStruct((b, t, w), dt) for dt in (BF16, BF16, BF16, F32, F32) + (BF16,) * 6],
        compiler_params=_cparams(("parallel", "parallel")),
        name="feat",
    )(zrw, prev_rows, next_rows, mu_prev, mu_next, k_k, k_a, r_k, w0, wup_pad, a0, aup_pad, g_up, ones_bd)


def _stack_bd(x, head0):
    zero = jnp.zeros_like(x)
    return jnp.concatenate([jnp.where(head0, x, zero), jnp.where(head0, zero, x)], axis=0)


def _unstack_bd(x):
    c = x.shape[0] // 2
    return x[0:c] + x[c:2 * c]


def _prep_kernel(r_ref, v_ref, kk_ref, lwf_ref, lwb_ref, kmf_ref, kmb_ref, af_ref, ab_ref,
                 g1_o, g2_o, q1_o, q2_o):
    c = CHUNK
    n_pairs = r_ref.shape[2] // LANES
    lane = lax.broadcasted_iota(jnp.int32, (c, LANES), 1)
    head0 = lane < HEAD_DIM
    ti = lax.broadcasted_iota(jnp.int32, (c, c), 0)
    si = lax.broadcasted_iota(jnp.int32, (c, c), 1)
    t2 = lax.broadcasted_iota(jnp.int32, (c, LANES), 0)
    s2 = lane % c
    eye = s2 == t2
    levels = int(math.log2(c))
    pair = lambda x, p: x[:, p * LANES:(p + 1) * LANES]
    bd16 = lambda x: _stack_bd(x, head0).astype(BF16)
    wc, a_c, r_c, b_c, k_c, v_bd, past, incl, dest = [], [], [], [], [], [], [], [], []
    dirs = ((lwf_ref, kmf_ref, af_ref), (lwb_ref, kmb_ref, ab_ref))
    for ch, (d, (lw_ref, km_ref, a_ref)) in [(ch, dr) for ch in range(PREP_CHUNKS) for dr in enumerate(dirs)]:
        rows = slice(ch * c, (ch + 1) * c)
        r = r_ref[0, rows].astype(F32)
        kk = kk_ref[0, rows].astype(F32)
        v_pairs = [bd16(pair(v_ref[0, rows].astype(F32), p)) for p in range(n_pairs)]
        lw = lw_ref[0, rows]
        if d == 0:
            tri = (si <= ti).astype(F32)
            past_d = s2 < t2
            last = c - 1
        else:
            tri = (si >= ti).astype(F32)
            past_d = s2 > t2
            last = 0
        incl_d = past_d | eye
        cs = _dot(tri, lw, HIGHEST)
        e_pos = jnp.exp(cs)
        e_neg = jnp.exp(-cs)
        a_all = -kk * jnp.exp(cs - lw)
        b_all = kk * a_ref[0, rows].astype(F32) * e_neg
        k_all = km_ref[0, rows].astype(F32) * e_neg
        r_all = r * e_pos
        wc_all = e_pos[last:last + 1, :]
        for p in range(n_pairs):
            wc.append(pair(wc_all, p))
            a_c.append(pair(a_all, p))
            r_c.append(pair(r_all, p))
            b_c.append(pair(b_all, p))
            k_c.append(pair(k_all, p))
            v_bd.append(v_pairs[p])
            past.append(past_d)
            incl.append(incl_d)
            dest.append((d, ch, p))
    cs_ = range(len(dest))
    w = LANES
    b_bd = [_stack_bd(b_c[i], head0) for i in cs_]
    k_bd = [_stack_bd(k_c[i], head0) for i in cs_]
    gm = [_dot_nt(jnp.concatenate([a_c[i], r_c[i]], axis=0).astype(BF16),
                  jnp.concatenate([b_bd[i], k_bd[i]], axis=0).astype(BF16)) for i in cs_]
    aab = [jnp.where(past[i], gm[i][0:c, 0:w], 0.0) for i in cs_]
    aak = [jnp.where(past[i], gm[i][0:c, w:2 * w], 0.0) for i in cs_]
    mrb = [jnp.where(incl[i], gm[i][c:2 * c, 0:w], 0.0) for i in cs_]
    mrk = [jnp.where(incl[i], gm[i][c:2 * c, w:2 * w], 0.0) for i in cs_]
    bw_t = [_unstack_bd(jnp.transpose(b_bd[i] * wc[i])) for i in cs_]
    kw_t = [_unstack_bd(jnp.transpose(k_bd[i] * wc[i])) for i in cs_]
    xv = [_dot(jnp.concatenate([aak[i], mrk[i], kw_t[i]], axis=0).astype(BF16), v_bd[i]) for i in cs_]
    tinv = [jnp.where(eye, 1.0, aab[i]) for i in cs_]
    xp = [_dot(aab[i].astype(BF16), bd16(aab[i])) for i in cs_]
    for _ in range(2, levels):
        res = [_dot(jnp.concatenate([xp[i], tinv[i]], axis=0).astype(BF16), bd16(xp[i])) for i in cs_]
        xp = [res[i][0:c] for i in cs_]
        tinv = [tinv[i] + res[i][c:2 * c] for i in cs_]
    tinv = [tinv[i] + _dot(tinv[i].astype(BF16), bd16(xp[i])) for i in cs_]
    z = [_dot(tinv[i].astype(BF16), jnp.concatenate([bd16(a_c[i]), bd16(xv[i][0:c])], axis=1)) for i in cs_]
    z_bd = [jnp.concatenate([bd16(z[i][:, 0:w]), bd16(z[i][:, w:2 * w])], axis=1) for i in cs_]
    qg = [_dot(jnp.concatenate([mrb[i], bw_t[i]], axis=0).astype(BF16), z_bd[i]) for i in cs_]
    for i, (d, ch, p) in enumerate(dest):
        q1_o[d, ch, p] = r_c[i] + qg[i][0:c, 0:w]
        q2_o[d, ch, p] = qg[i][0:c, w:2 * w] + xv[i][c:2 * c]
        g1_o[d, ch, p] = jnp.where(eye, wc[i], 0.0) + qg[i][c:2 * c, 0:w]
        g2_o[d, ch, p] = qg[i][c:2 * c, w:2 * w] + xv[i][2 * c:3 * c]


def _prep_call(r, v, kk, lwf, lwb, kmf, kmb, af, ab):
    b, t, w = r.shape
    n_pairs = w // LANES
    nch = t // CHUNK
    assert nch % PREP_CHUNKS == 0
    in_spec = pl.BlockSpec((1, PREP_CHUNKS * CHUNK, w), lambda bi, j: (bi, j, 0))
    out_spec = pl.BlockSpec((2, PREP_CHUNKS, n_pairs, CHUNK, LANES), lambda bi, j: (0, j, bi, 0, 0))
    out_sd = jax.ShapeDtypeStruct((2, nch, b * n_pairs, CHUNK, LANES), F32)
    return pl.pallas_call(
        _prep_kernel,
        grid=(b, nch // PREP_CHUNKS),
        in_specs=[in_spec] * 9,
        out_specs=[out_spec] * 4,
        out_shape=[out_sd] * 4,
        compiler_params=_cparams(("parallel", "parallel")),
        name="prep",
    )(r, v, kk, lwf, lwb, kmf, kmb, af, ab)


def _scan_kernel(g1f, g2f, q1f, q2f, g1b, g2b, q1b, q2b, yf_o, yb_o, hf_s, hb_s):
    @pl.when(pl.program_id(0) == 0)
    def _():
        hf_s[...] = jnp.zeros_like(hf_s)
        hb_s[...] = jnp.zeros_like(hb_s)

    lane = lax.broadcasted_iota(jnp.int32, (CHUNK, LANES), 1)
    head0 = lane < HEAD_DIM
    n_pairs = hf_s.shape[0]
    c = CHUNK
    for g1, g2, q1, q2, y_o, h_s in ((g1f, g2f, q1f, q2f, yf_o, hf_s), (g1b, g2b, q1b, q2b, yb_o, hb_s)):
        for p in range(n_pairs):
            lhs = jnp.concatenate([g1[0, 0, p], q1[0, 0, p]], axis=0)
            res = _dot_split3(lhs, _stack_bd(h_s[p], head0))
            h_s[p] = res[0:c] + g2[0, 0, p]
            y_o[0, p] = res[c:2 * c] + q2[0, 0, p]


def _scan_call(g1, g2, q1, q2, n_ctx_chunks):
    _, nch, npair, c, lanes = g1.shape
    n_lat_chunks = nch - n_ctx_chunks
    fwd = lambda s: (0, s, 0, 0, 0)
    bwd_chunk = lambda s: jnp.where(s < n_ctx_chunks, n_ctx_chunks - 1 - s, 2 * n_ctx_chunks + n_lat_chunks - 1 - s)
    bwd = lambda s: (1, bwd_chunk(s), 0, 0, 0)
    blk = (1, 1, npair, c, lanes)
    y_sd = jax.ShapeDtypeStruct((nch, npair, c, lanes), F32)
    return pl.pallas_call(
        _scan_kernel,
        grid=(nch,),
        in_specs=[pl.BlockSpec(blk, fwd)] * 4 + [pl.BlockSpec(blk, bwd)] * 4,
        out_specs=[pl.BlockSpec((1, npair, c, lanes), lambda s: (s, 0, 0, 0)),
                   pl.BlockSpec((1, npair, c, lanes), lambda s: (bwd_chunk(s), 0, 0, 0))],
        out_shape=[y_sd, y_sd],
        scratch_shapes=[pltpu.VMEM((npair, c, lanes), F32), pltpu.VMEM((npair, c, lanes), F32)],
        compiler_params=_cparams(("arbitrary",)),
        name="scan",
    )(g1, g2, q1, q2, g1, g2, q1, q2)


def _merge_kernel(yf_ref, yb_ref, bonus_ref, g_ref, zg_ref, yna_ref, x_ref, gt1_ref, sh2_ref, sc2_ref,
                  wpa_ref, wpr_ref, wo_ref, gnw_ref, gnb_ref, l1g_ref, l1b_ref, wrt_ref, ones_ref,
                  h1_o, u2_o, aff_o):
    tm = x_ref.shape[1]
    n_pairs = yf_ref.shape[1]
    ys = yf_ref[...] + yb_ref[...]
    y = jnp.concatenate([ys[:, p].reshape(tm, LANES) for p in range(n_pairs)], axis=1)
    ones_bd = ones_ref[...]
    inv = 1.0 / HEAD_DIM
    mu = _head_sum(y, ones_bd) * inv
    yc = y - mu
    var = _head_sum(yc * yc, ones_bd) * inv
    yn = yc * lax.rsqrt(var + GN_EPS) * gnw_ref[...] + gnb_ref[...]
    y_rw = (yn + bonus_ref[0].astype(F32)) * g_ref[0].astype(F32)
    gates = jax.nn.sigmoid(zg_ref[0].astype(F32))
    d = x_ref.shape[2]
    m1 = gates[:, 0:d] * _dot(yna_ref[0], wpa_ref[...]) + gates[:, d:2 * d] * _dot(y_rw.astype(BF16), wpr_ref[...])
    m = _dot(m1.astype(BF16), wo_ref[...])
    h1 = _ln(ALPHA * x_ref[0] + gt1_ref[0] * m, LN_EPS) * l1g_ref[...] + l1b_ref[...]
    h1_o[0] = h1
    u2 = _ln(h1, LN_EPS) * (1.0 + sc2_ref[0]) + sh2_ref[0]
    u2_o[0] = u2.astype(BF16)
    logits = _dot_nt(wrt_ref[...], u2, HIGHEST)
    mx = jnp.max(logits, axis=0, keepdims=True)
    e = jnp.exp(logits - mx)
    aff_o[0] = e / jnp.sum(e, axis=0, keepdims=True)


def _merge_call(yf, yb, bonus, g, zgate, y_na, x, gt1, sh2, sc2, w_pa, w_pr, w_o, gn_w, gn_b, l1g, l1b, w_rt,
                ones_bd, n_ctx):
    b, n, d = x.shape
    tm = ROW_TILE
    cpt = tm // CHUNK
    n_pairs = RW_WIDTH // LANES
    ctx_tiles = n_ctx // tm
    e = w_rt.shape[0]
    y_spec = pl.BlockSpec((cpt, n_pairs, CHUNK, LANES), lambda bi, i: (ctx_tiles + i, bi, 0, 0))
    full = lambda shape: pl.BlockSpec(shape, lambda bi, i: (0,) * len(shape))
    modv = pl.BlockSpec((1, 1, d), lambda bi, i: (bi, 0, 0))
    return pl.pallas_call(
        _merge_kernel,
        grid=(b, n // tm),
        in_specs=[y_spec, y_spec,
                  pl.BlockSpec((1, tm, RW_WIDTH), lambda bi, i: (bi, ctx_tiles + i, 0)),
                  pl.BlockSpec((1, tm, RW_WIDTH), lambda bi, i: (bi, ctx_tiles + i, 0)),
                  pl.BlockSpec((1, tm, 2 * d), lambda bi, i: (bi, ctx_tiles + i, 0)),
                  pl.BlockSpec((1, tm, NA_WIDTH), lambda bi, i: (bi, i, 0)),
                  pl.BlockSpec((1, tm, d), lambda bi, i: (bi, i, 0)),
                  modv, modv, modv,
                  full((NA_WIDTH, d)), full((RW_WIDTH, d)), full((d, d)),
                  full((1, RW_WIDTH)), full((1, RW_WIDTH)), full((1, d)), full((1, d)), full((e, d)),
                  full((RW_WIDTH, RW_WIDTH))],
        out_specs=[pl.BlockSpec((1, tm, d), lambda bi, i: (bi, i, 0)),
                   pl.BlockSpec((1, tm, d), lambda bi, i: (bi, i, 0)),
                   pl.BlockSpec((1, e, tm), lambda bi, i: (bi, 0, i))],
        out_shape=[jax.ShapeDtypeStruct((b, n, d), F32), jax.ShapeDtypeStruct((b, n, d), BF16),
                   jax.ShapeDtypeStruct((b, e, n), F32)],
        compiler_params=_cparams(("parallel", "parallel")),
        name="merge",
    )(yf, yb, bonus, g, zgate, y_na, x, gt1, sh2, sc2, w_pa, w_pr, w_o, gn_w, gn_b, l1g, l1b, w_rt, ones_bd)


def _ffn_kernel(x_ref, gate_ref, w1_ref, w3_ref, w2_ref, o_ref, acc_ref):
    f = pl.program_id(2)

    @pl.when(f == 0)
    def _():
        acc_ref[...] = jnp.zeros_like(acc_ref)

    xs = x_ref[0, 0]
    a = _dot(xs, w1_ref[0].astype(BF16))
    bb = _dot(xs, w3_ref[0].astype(BF16))
    hdn = (a * jax.nn.sigmoid(a) * bb).astype(BF16)
    acc_ref[...] += _dot(hdn, w2_ref[0].astype(BF16))

    @pl.when(f == pl.num_programs(2) - 1)
    def _():
        o_ref[0, 0] = acc_ref[...] * gate_ref[0, 0]


def _ffn_call(xs, gate, w1, w3, w2):
    b, e, c, d = xs.shape
    fdim = w1.shape[2]
    tf = FFN_TILE
    return pl.pallas_call(
        _ffn_kernel,
        grid=(e, b, fdim // tf),
        in_specs=[pl.BlockSpec((1, 1, c, d), lambda ei, bi, f: (bi, ei, 0, 0)),
                  pl.BlockSpec((1, 1, c, 1), lambda ei, bi, f: (bi, ei, 0, 0)),
                  pl.BlockSpec((1, d, tf), lambda ei, bi, f: (ei, 0, f)),
                  pl.BlockSpec((1, d, tf), lambda ei, bi, f: (ei, 0, f)),
                  pl.BlockSpec((1, tf, d), lambda ei, bi, f: (ei, f, 0))],
        out_specs=pl.BlockSpec((1, 1, c, d), lambda ei, bi, f: (bi, ei, 0, 0)),
        out_shape=jax.ShapeDtypeStruct((b, e, c, d), F32),
        scratch_shapes=[pltpu.VMEM((c, d), F32)],
        compiler_params=_cparams(("parallel", "parallel", "arbitrary")),
        name="ffn",
    )(xs, gate, w1, w3, w2)


def _final_kernel(h_ref, moe_ref, gt2_ref, g_ref, b_ref, o_ref):
    o_ref[0] = _ln(ALPHA * h_ref[0] + gt2_ref[0] * moe_ref[0], LN_EPS) * g_ref[...] + b_ref[...]


def _final_call(h1, moe, gt2, l2g, l2b):
    b, n, d = h1.shape
    tm = ROW_TILE
    row = pl.BlockSpec((1, tm, d), lambda bi, i: (bi, i, 0))
    vec = pl.BlockSpec((1, d), lambda bi, i: (0, 0))
    return pl.pallas_call(
        _final_kernel,
        grid=(b, n // tm),
        in_specs=[row, row, pl.BlockSpec((1, 1, d), lambda bi, i: (bi, 0, 0)), vec, vec],
        out_specs=row,
        out_shape=jax.ShapeDtypeStruct((b, n, d), F32),
        compiler_params=_cparams(("parallel", "parallel")),
        name="final",
    )(h1, moe, gt2, l2g, l2b)


def _rope_tables(n_ctx, n_lat):
    nf = HEAD_DIM // 4
    t = np.arange(n_lat)
    pos = np.stack([t // GRID_W, t % GRID_W], axis=-1).astype(np.float32)
    inv_freq = jnp.power(jnp.float32(ROPE_BASE), -jnp.arange(nf, dtype=F32) / nf)
    ang = jnp.asarray(pos)[:, :, None] * inv_freq
    cos = jnp.cos(ang)
    sin = jnp.sin(ang)
    cos_h = jnp.concatenate([cos[:, 0], cos[:, 0], cos[:, 1], cos[:, 1]], axis=-1)
    sin_h = jnp.concatenate([-sin[:, 0], sin[:, 0], -sin[:, 1], sin[:, 1]], axis=-1)
    cos_t = jnp.concatenate([jnp.ones((n_ctx, HEAD_DIM), F32), cos_h], axis=0)
    sin_t = jnp.concatenate([jnp.zeros((n_ctx, HEAD_DIM), F32), sin_h], axis=0)
    return jnp.tile(cos_t, (1, LANES // HEAD_DIM)), jnp.tile(sin_t, (1, LANES // HEAD_DIM))


def kernel(x, c, ctx, c_ctx, w_mod, b_mod, w_in, rpb, mu_prev, mu_next, w0, w_up, a0, a_up, g_up, k_k, k_a, r_k,
           gn_w, gn_b, w_pa, w_pr, w_o, ln1_g, ln1_b, w_router, w_e1, w_e3, w_e2, ln2_g, ln2_b):
    assert w_mod.shape[0] == DEPTH
    b, n, d = x.shape
    n_ctx = ctx.shape[1]
    assert n_ctx % ROW_TILE == 0 and n % ROW_TILE == 0 and (n // GRID_W) % NA_ROWS == 0
    t = n_ctx + n
    w = RW_WIDTH

    mod_rows = 8 * ((b + 1 + 7) // 8)
    cvec = jnp.zeros((mod_rows, d), F32).at[:b].set(c).at[b].set(c_ctx)
    mod = _mod_call(cvec, w_mod[0], b_mod[0][None])[:b + 1]
    sh1, sc1, gt1, sh2, sc2, gt2 = [mod[:, None, i * d:(i + 1) * d] for i in range(6)]

    cos_tab, sin_tab = _rope_tables(n_ctx, n)
    q_all, k_all, v_all, zrw, zgate, first_rows, last_rows = _proj_call(
        ctx, x, sh1, sc1, w_in[0].astype(BF16), cos_tab, sin_tab)

    y_na = _natten_call(q_all, k_all, v_all, _natten_bias_table(rpb[0]), n_ctx, n)

    nt = t // ROW_TILE
    ctx_tiles = n_ctx // ROW_TILE
    zero_row = jnp.zeros((b, 1, 1, RW_COLS), F32)
    prev_rows = jnp.concatenate([zero_row, last_rows[:, :nt - 1]], axis=1)
    next_rows = jnp.concatenate([first_rows[:, 1:], zero_row], axis=1)
    seq_start = (np.arange(nt) == 0) | (np.arange(nt) == ctx_tiles)
    seq_end = (np.arange(nt) == ctx_tiles - 1) | (np.arange(nt) == nt - 1)
    prev_rows = jnp.where(jnp.asarray(seq_start)[None, :, None, None], 0.0, prev_rows)
    next_rows = jnp.where(jnp.asarray(seq_end)[None, :, None, None], 0.0, next_rows)
    zpad = jnp.zeros((DECAY_LORA, w), F32)
    wup_pad = jnp.stack([jnp.concatenate([w_up[0, 0], zpad], 0), jnp.concatenate([zpad, w_up[0, 1]], 0)])
    aup_pad = jnp.stack([jnp.concatenate([a_up[0, 0], zpad], 0), jnp.concatenate([zpad, a_up[0, 1]], 0)])
    ones_bd = jnp.asarray(np.kron(np.eye(w // HEAD_DIM), np.ones((HEAD_DIM, HEAD_DIM))), BF16)
    r, v, kk, lwf, lwb, kmf, kmb, af, ab, g, bonus = _feat_call(
        zrw, prev_rows, next_rows, mu_prev, mu_next, k_k, k_a, r_k[0].reshape(1, w), w0[0][:, None, :], wup_pad,
        a0[0][:, None, :], aup_pad.astype(BF16), g_up[0].astype(BF16), ones_bd)

    g1, g2, q1, q2 = _prep_call(r, v, kk, lwf, lwb, kmf, kmb, af, ab)
    yf, yb = _scan_call(g1, g2, q1, q2, n_ctx // CHUNK)

    h1, u2, aff_t = _merge_call(yf, yb, bonus, g, zgate, y_na, x, gt1[:b], sh2[:b], sc2[:b],
                                w_pa[0].astype(BF16), w_pr[0].astype(BF16), w_o[0].astype(BF16),
                                gn_w, gn_b, ln1_g, ln1_b, jnp.transpose(w_router[0]), ones_bd, n_ctx)

    cap = CAPACITY_FACTOR * n // N_EXPERTS
    gate, idx = lax.top_k(aff_t, cap)
    flat_idx = (idx + (jnp.arange(b, dtype=jnp.int32) * n)[:, None, None]).reshape(-1)
    xs = jnp.take(u2.reshape(b * n, d), flat_idx, axis=0).reshape(b, N_EXPERTS, cap, d)
    ye = _ffn_call(xs, gate[..., None], w_e1[0], w_e3[0], w_e2[0])
    moe = jnp.zeros((b * n, d), F32).at[flat_idx].add(ye.reshape(-1, d)).reshape(b, n, d)

    return _final_call(h1, moe, gt2[:b], ln2_g, ln2_b)
```
